```python
import math
import jax, jax.numpy as jnp
from jax import lax
import numpy as np

D_MODEL = 2048
BATCH = 4
SEQ = 2048
DEPTH = 1

D_MIX = D_MODEL
HEAD_DIM = 64
D_ATTN = D_MIX // 2
N_Q_HEADS = D_ATTN // HEAD_DIM
N_KV_HEADS = 2
Q_PER_KV = N_Q_HEADS // N_KV_HEADS
D_KV = N_KV_HEADS * HEAD_DIM
WINDOW = 128
BLOCK = WINDOW
D_SGU = D_MIX - D_ATTN
SGU_GROUPS = 8
SGU_GROUP_DIM = D_SGU // SGU_GROUPS
CHUNK = 128
D_IN = D_ATTN + 2 * D_KV + D_ATTN + 3 * D_SGU
EPS = 1e-6

kernel_name = "hybrid_swa_sink_gmlp_parallel_heads"


def rms_norm(x, g):
    xf = x.astype(jnp.float32)
    y = xf * lax.rsqrt(jnp.mean(xf * xf, axis=-1, keepdims=True) + EPS)
    return (y * g.astype(jnp.float32)).astype(x.dtype)


def layer_norm(x, g, b):
    xf = x.astype(jnp.float32)
    mu = jnp.mean(xf, axis=-1, keepdims=True)
    var = jnp.mean(jnp.square(xf - mu), axis=-1, keepdims=True)
    y = (xf - mu) * lax.rsqrt(var + EPS)
    return (y * g.astype(jnp.float32) + b.astype(jnp.float32)).astype(x.dtype)


def sliding_window_sink_attention(q, k, v, sinks):
    B, S = q.shape[0], q.shape[1]
    nb = S // BLOCK
    qb = q.reshape(B, nb, BLOCK, N_KV_HEADS, Q_PER_KV, HEAD_DIM)
    kb = k.reshape(B, nb, BLOCK, N_KV_HEADS, HEAD_DIM)
    vb = v.reshape(B, nb, BLOCK, N_KV_HEADS, HEAD_DIM)
    pad = ((0, 0), (1, 0), (0, 0), (0, 0), (0, 0))
    k_ext = jnp.concatenate([jnp.pad(kb, pad)[:, :-1], kb], axis=2)
    v_ext = jnp.concatenate([jnp.pad(vb, pad)[:, :-1], vb], axis=2)
    scale = 1.0 / math.sqrt(HEAD_DIM)
    scores = jnp.einsum('bnqhgd,bnshd->bnhgqs', qb, k_ext).astype(jnp.float32) * scale
    qpos = jnp.arange(BLOCK)[:, None] + BLOCK
    kpos = jnp.arange(2 * BLOCK)[None, :]
    dist = qpos - kpos
    band = (dist >= 0) & (dist < WINDOW)
    has_prev = (jnp.arange(nb) > 0)[:, None, None] | (kpos >= BLOCK)[None]
    valid = band[None] & has_prev
    scores = jnp.where(valid[None, :, None, None], scores, -jnp.inf)
    sink = sinks.astype(jnp.float32).reshape(N_KV_HEADS, Q_PER_KV)
    sink = jnp.broadcast_to(sink[None, None, :, :, None, None], scores.shape[:-1] + (1,))
    probs = jax.nn.softmax(jnp.concatenate([scores, sink], axis=-1), axis=-1)[..., :-1]
    out = jnp.einsum('bnhgqs,bnshd->bnqhgd', probs.astype(v.dtype), v_ext)
    return out.reshape(B, S, N_Q_HEADS * HEAD_DIM)


def chunked_spatial_gating(u, v, ln_g, ln_b, w_s, b_s):
    B, S = u.shape[0], u.shape[1]
    nc = S // CHUNK
    vn = layer_norm(v, ln_g, ln_b).reshape(B, nc, CHUNK, SGU_GROUPS, SGU_GROUP_DIM)
    causal = jnp.tril(jnp.ones((CHUNK, CHUNK), dtype=w_s.dtype))
    w = w_s * causal[None]
    mixed = jnp.einsum('gts,bnsgc->bntgc', w, vn) + b_s.T[None, None, :, :, None]
    return u * mixed.reshape(B, S, D_SGU)


def setup_inputs(seed: int = 0) -> dict:
    key = jax.random.key(seed)
    ks = jax.random.split(key, 14)
    f32 = jnp.float32
    x = jax.random.normal(ks[0], (BATCH, SEQ, D_MODEL), f32)
    c = jax.random.normal(ks[1], (BATCH, D_MODEL), f32)
    norm_g = 1.0 + 0.05 * jax.random.normal(ks[2], (DEPTH, D_MODEL), f32)
    w_ada = 0.5 * D_MODEL ** -0.5 * jax.random.normal(ks[3], (DEPTH, D_MODEL, 3 * D_MODEL), f32)
    b_ada = 0.02 * jax.random.normal(ks[4], (DEPTH, 3 * D_MODEL), f32)
    w_in = D_MODEL ** -0.5 * jax.random.normal(ks[5], (DEPTH, D_MODEL, D_IN), f32)
    attn_sinks = 0.5 * jax.random.normal(ks[6], (DEPTH, N_Q_HEADS), f32)
    sgu_ln_g = 1.0 + 0.05 * jax.random.normal(ks[7], (DEPTH, D_SGU), f32)
    sgu_ln_b = 0.02 * jax.random.normal(ks[8], (DEPTH, D_SGU), f32)
    sgu_w = CHUNK ** -0.5 * jax.random.normal(ks[9], (DEPTH, SGU_GROUPS, CHUNK, CHUNK), f32)
    sgu_b = 1.0 + 0.1 * jax.random.normal(ks[10], (DEPTH, SGU_GROUPS, CHUNK), f32)
    w_out = D_MIX ** -0.5 * jax.random.normal(ks[11], (DEPTH, D_MIX, D_MODEL), f32)
    final_g = 1.0 + 0.05 * jax.random.normal(ks[12], (D_MODEL,), f32)
    return {"x": x, "c": c, "norm_g": norm_g, "w_ada": w_ada, "b_ada": b_ada,
            "w_in": w_in, "attn_sinks": attn_sinks, "sgu_ln_g": sgu_ln_g,
            "sgu_ln_b": sgu_ln_b, "sgu_w": sgu_w, "sgu_b": sgu_b, "w_out": w_out,
            "final_g": final_g}


def reference(x, c, norm_g, w_ada, b_ada, w_in, attn_sinks, sgu_ln_g, sgu_ln_b,
              sgu_w, sgu_b, w_out, final_g):
    B, S = x.shape[0], x.shape[1]
    splits = np.cumsum([D_ATTN, D_KV, D_KV, D_ATTN, D_SGU, D_SGU])
    c_act = jax.nn.silu(c)
    for l in range(DEPTH):
        mod = c_act @ w_ada[l] + b_ada[l]
        shift, scale, gate = jnp.split(mod, 3, axis=-1)
        h = rms_norm(x, norm_g[l]) * (1.0 + scale[:, None, :]) + shift[:, None, :]
        z = h @ w_in[l]
        q, k, v, g_attn, u, v_s, g_sgu = jnp.split(z, splits, axis=-1)
        attn = sliding_window_sink_attention(
            q.reshape(B, S, N_Q_HEADS, HEAD_DIM),
            k.reshape(B, S, N_KV_HEADS, HEAD_DIM),
            v.reshape(B, S, N_KV_HEADS, HEAD_DIM),
            attn_sinks[l]) * jax.nn.silu(g_attn)
        sgu = chunked_spatial_gating(u, v_s, sgu_ln_g[l], sgu_ln_b[l],
                                     sgu_w[l], sgu_b[l]) * jax.nn.silu(g_sgu)
        y = jnp.concatenate([attn, sgu], axis=-1) @ w_out[l]
        x = x + gate[:, None, :] * y
    return rms_norm(x, final_g)
```

```python
import functools
import math

import jax
import jax.numpy as jnp
from jax import lax
from jax.experimental import pallas as pl
from jax.experimental.pallas import tpu as pltpu

HEAD_DIM = 64
N_KV_HEADS = 2
WINDOW = 128
SGU_GROUPS = 8
EPS = 1e-6

F32 = jnp.float32
BF16 = jnp.bfloat16

LANES = 128
VMEM_LIMIT_BYTES = 56 * 1024 * 1024


def _silu(v):
    return v / (1.0 + jnp.exp(-v))


def _adaln_kernel(c_ref, w_ref, b_ref, o_ref):
    c_act = _silu(c_ref[...])
    acc = jnp.dot(c_act.astype(BF16), w_ref[...].astype(BF16), preferred_element_type=F32)
    o_ref[...] = acc + b_ref[...]


def _adaln(c_pad, w_ada, b_ada, tn):
    rows, d = c_pad.shape
    n = w_ada.shape[1]
    return pl.pallas_call(
        _adaln_kernel,
        grid=(n // tn,),
        in_specs=[
            pl.BlockSpec((rows, d), lambda j: (0, 0)),
            pl.BlockSpec((d, tn), lambda j: (0, j)),
            pl.BlockSpec((1, tn), lambda j: (0, j)),
        ],
        out_specs=pl.BlockSpec((rows, tn), lambda j: (0, j)),
        out_shape=jax.ShapeDtypeStruct((rows, n), F32),
        compiler_params=pltpu.CompilerParams(
            dimension_semantics=("arbitrary",), vmem_limit_bytes=VMEM_LIMIT_BYTES),
        name="adaln",
    )(c_pad, w_ada, b_ada)


def _inproj_kernel(x_ref, mod_ref, g_ref, w_ref, z_ref, h_ref, *, row_chunk):
    tm = x_ref.shape[0]

    @pl.when(pl.program_id(1) == 0)
    def _():
        shift = mod_ref[0, 0:1, :]
        gain = g_ref[...] * (1.0 + mod_ref[0, 1:2, :])
        for r in range(0, tm, row_chunk):
            x = x_ref[r:r + row_chunk, :]
            ms = jnp.mean(x * x, axis=-1, keepdims=True)
            y = x * lax.rsqrt(ms + EPS)
            h_ref[r:r + row_chunk, :] = (y * gain + shift).astype(BF16)

    z_ref[...] = jnp.dot(h_ref[...], w_ref[...], preferred_element_type=F32).astype(BF16)


def _inproj(x2, mod3, norm_g, w_in_bf, seq, tm, tn):
    m, d = x2.shape
    n = w_in_bf.shape[1]
    tiles_per_seq = seq // tm
    return pl.pallas_call(
        functools.partial(_inproj_kernel, row_chunk=128),
        grid=(m // tm, n // tn),
        in_specs=[
            pl.BlockSpec((tm, d), lambda i, j: (i, 0)),
            pl.BlockSpec((1, 3, d), lambda i, j: (i // tiles_per_seq, 0, 0)),
            pl.BlockSpec((1, d), lambda i, j: (0, 0)),
            pl.BlockSpec((d, tn), lambda i, j: (0, j)),
        ],
        out_specs=pl.BlockSpec((tm, tn), lambda i, j: (i, j)),
        out_shape=jax.ShapeDtypeStruct((m, n), BF16),
        scratch_shapes=[pltpu.VMEM((tm, d), BF16)],
        compiler_params=pltpu.CompilerParams(
            dimension_semantics=("arbitrary", "arbitrary"), vmem_limit_bytes=VMEM_LIMIT_BYTES),
        name="inproj",
    )(x2, mod3, norm_g, w_in_bf)


def _mix_kernel(sinks_ref, z_ref, kvp_ref, x_ref, mod_ref, lng_ref, lnb_ref, sw_ref, sbt_ref,
                wout_ref, fg_ref, o_ref, a_ref, *, d_attn, d_sgu):
    t = z_ref.shape[0]
    blk = WINDOW
    d_kv = N_KV_HEADS * HEAD_DIM
    pairs_per_kv = d_attn // LANES // N_KV_HEADS
    rows4 = pairs_per_kv * blk

    off_k = d_attn
    off_v = off_k + d_kv
    off_ga = off_v + d_kv
    off_u = off_ga + d_attn
    off_vs = off_u + d_sgu
    off_gs = off_vs + d_sgu

    lane2 = lax.broadcasted_iota(jnp.int32, (2 * blk, LANES), 1)
    lo2 = lane2 < HEAD_DIM
    row = lax.broadcasted_iota(jnp.int32, (blk, blk), 0)
    col = lax.broadcasted_iota(jnp.int32, (blk, blk), 1)
    tri = col <= row
    row4 = lax.broadcasted_iota(jnp.int32, (rows4, blk), 0) & (blk - 1)
    col4 = lax.broadcasted_iota(jnp.int32, (rows4, blk), 1)
    tri4 = col4 <= row4
    lo4 = col4 < HEAD_DIM
    scale = 1.0 / math.sqrt(HEAD_DIM)

    first_in_seq = pl.program_id(1) == 0
    pen0 = jnp.where(first_in_seq, -jnp.inf, 0.0).astype(F32)

    for j in range(t // blk):
        r0 = j * blk
        rows = slice(r0, r0 + blk)

        k_cur = z_ref[rows, off_k:off_k + d_kv]
        v_cur = z_ref[rows, off_v:off_v + d_kv]
        if j == 0:
            k_prev = kvp_ref[:, 0:d_kv]
            v_prev = kvp_ref[:, d_kv:2 * d_kv]
        else:
            k_prev = z_ref[r0 - blk:r0, off_k:off_k + d_kv]
            v_prev = z_ref[r0 - blk:r0, off_v:off_v + d_kv]
        kext = jnp.concatenate([k_prev, k_cur], axis=0).astype(F32) * scale
        vext = jnp.concatenate([v_prev, v_cur], axis=0).astype(F32)
        kroll = pltpu.roll(kext, HEAD_DIM, 1)
        vroll = pltpu.roll(vext, HEAD_DIM, 1)
        zero2 = jnp.zeros_like(kext)
        k_lo = [jnp.where(lo2, kext, zero2), jnp.where(lo2, kroll, zero2)]
        k_hi = [jnp.where(lo2, zero2, kroll), jnp.where(lo2, zero2, kext)]
        v_lo = [jnp.where(lo2, vext, zero2), jnp.where(lo2, vroll, zero2)]
        v_hi = [jnp.where(lo2, zero2, vroll), jnp.where(lo2, zero2, vext)]

        for kv in range(N_KV_HEADS):
            p0 = kv * pairs_per_kv
            kbd = jnp.concatenate([k_lo[kv], k_hi[kv]], axis=0).astype(BF16)
            vbd = [v_lo[kv].astype(BF16), v_hi[kv].astype(BF16)]
            q4 = jnp.concatenate(
                [z_ref[rows, (p0 + p) * LANES:(p0 + p + 1) * LANES] for p in range(pairs_per_kv)],
                axis=0)
            s = lax.dot_general(q4, kbd, (((1,), (1,)), ((), ())),
                                preferred_element_type=F32)
            acc = None
            for e in range(2):
                s_prev = s[:, 2 * blk * e:2 * blk * e + blk]
                s_cur = s[:, 2 * blk * e + blk:2 * blk * (e + 1)]
                if j == 0:
                    s_prev = s_prev + pen0
                sc = jnp.where(tri4, s_cur, s_prev)
                sink = jnp.concatenate(
                    [jnp.full((blk, 1), sinks_ref[2 * (p0 + p) + e], F32)
                     for p in range(pairs_per_kv)], axis=0)
                m = jnp.maximum(jnp.max(sc, axis=-1, keepdims=True), sink)
                pr = jnp.exp(sc - m)
                denom = jnp.sum(pr, axis=-1, keepdims=True) + jnp.exp(sink - m)
                zero4 = jnp.zeros_like(pr)
                pm = jnp.concatenate([jnp.where(tri4, zero4, pr), jnp.where(tri4, pr, zero4)],
                                     axis=1).astype(BF16)
                o_e = jnp.dot(pm, vbd[e], preferred_element_type=F32)
                o_e = o_e * (1.0 / denom)
                acc = o_e if acc is None else acc + o_e
            for p in range(pairs_per_kv):
                c0 = (p0 + p) * LANES
                g = z_ref[rows, off_ga + c0:off_ga + c0 + LANES].astype(F32)
                a_ref[rows, c0:c0 + LANES] = (acc[p * blk:(p + 1) * blk, :] * _silu(g)).astype(BF16)

        vs = z_ref[rows, off_vs:off_vs + d_sgu].astype(F32)
        mu = jnp.mean(vs, axis=-1, keepdims=True)
        dv = vs - mu
        var = jnp.mean(dv * dv, axis=-1, keepdims=True)
        vn = ((dv * lax.rsqrt(var + EPS)) * lng_ref[...] + lnb_ref[...]).astype(BF16)
        for g in range(SGU_GROUPS):
            c0 = g * LANES
            w = jnp.where(tri, sw_ref[g], 0.0).astype(BF16)
            mixed = jnp.dot(w, vn[:, c0:c0 + LANES], preferred_element_type=F32) + sbt_ref[:, g:g + 1]
            u = z_ref[rows, off_u + c0:off_u + c0 + LANES].astype(F32)
            gg = z_ref[rows, off_gs + c0:off_gs + c0 + LANES].astype(F32)
            a_ref[rows, d_attn + c0:d_attn + c0 + LANES] = ((u * mixed) * _silu(gg)).astype(BF16)

    y = jnp.dot(a_ref[...], wout_ref[...], preferred_element_type=F32)
    gate = mod_ref[0, 2:3, :]
    fg = fg_ref[...]
    for j in range(t // blk):
        rows = slice(j * blk, (j + 1) * blk)
        xn = x_ref[rows, :] + gate * y[rows, :]
        ms = jnp.mean(xn * xn, axis=-1, keepdims=True)
        o_ref[rows, :] = (xn * lax.rsqrt(ms + EPS)) * fg


def _mix(sinks, z, x2, mod3, ln_g, ln_b, sgu_w, sgu_bt, w_out_bf, final_g, seq, t, d_attn, d_sgu):
    m, d = x2.shape
    d_in = z.shape[1]
    d_mix = w_out_bf.shape[0]
    d_kv = N_KV_HEADS * HEAD_DIM
    steps = seq // t
    blocks_per_step = t // WINDOW
    blocks_per_seq = seq // WINDOW
    kv_col_block = d_attn // (2 * d_kv)
    assert d_attn % (2 * d_kv) == 0

    def prev_kv_map(b, i):
        return (b * blocks_per_seq + jnp.maximum(i * blocks_per_step - 1, 0), kv_col_block)

    return pl.pallas_call(
        functools.partial(_mix_kernel, d_attn=d_attn, d_sgu=d_sgu),
        grid=(m // seq, steps),
        in_specs=[
            pl.BlockSpec(memory_space=pltpu.SMEM),
            pl.BlockSpec((t, d_in), lambda b, i: (b * steps + i, 0)),
            pl.BlockSpec((WINDOW, 2 * d_kv), prev_kv_map),
            pl.BlockSpec((t, d), lambda b, i: (b * steps + i, 0)),
            pl.BlockSpec((1, 3, d), lambda b, i: (b, 0, 0)),
            pl.BlockSpec((1, d_sgu), lambda b, i: (0, 0)),
            pl.BlockSpec((1, d_sgu), lambda b, i: (0, 0)),
            pl.BlockSpec((SGU_GROUPS, WINDOW, WINDOW), lambda b, i: (0, 0, 0)),
            pl.BlockSpec((WINDOW, SGU_GROUPS), lambda b, i: (0, 0)),
            pl.BlockSpec((d_mix, d), lambda b, i: (0, 0)),
            pl.BlockSpec((1, d), lambda b, i: (0, 0)),
        ],
        out_specs=pl.BlockSpec((t, d), lambda b, i: (b * steps + i, 0)),
        out_shape=jax.ShapeDtypeStruct((m, d), F32),
        scratch_shapes=[pltpu.VMEM((t, d_mix), BF16)],
        compiler_params=pltpu.CompilerParams(
            dimension_semantics=("arbitrary", "arbitrary"), vmem_limit_bytes=VMEM_LIMIT_BYTES),
        name="mix",
    )(sinks, z, z, x2, mod3, ln_g, ln_b, sgu_w, sgu_bt, w_out_bf, final_g)


def kernel(x, c, norm_g, w_ada, b_ada, w_in, attn_sinks, sgu_ln_g, sgu_ln_b, sgu_w, sgu_b, w_out, final_g):
    batch, seq, d = x.shape
    depth = norm_g.shape[0]
    n_q_heads = attn_sinks.shape[1]
    d_attn = n_q_heads * HEAD_DIM
    d_sgu = sgu_ln_g.shape[1]
    assert sgu_w.shape[1:] == (SGU_GROUPS, WINDOW, WINDOW) and d_sgu == SGU_GROUPS * LANES

    c_pad = jnp.pad(c, ((0, 8 - batch % 8), (0, 0))) if batch % 8 else c
    x2 = x.reshape(batch * seq, d)
    for l in range(depth):
        mod = _adaln(c_pad, w_ada[l], b_ada[l].reshape(1, -1), tn=512)[:batch]
        mod3 = mod.reshape(batch, 3, d)
        z = _inproj(x2, mod3, norm_g[l].reshape(1, d), w_in[l].astype(BF16), seq, tm=1024, tn=768)
        last = l == depth - 1
        assert last, "stacked layers need the un-normalised residual stream between layers"
        x2 = _mix(attn_sinks[l], z, x2, mod3, sgu_ln_g[l].reshape(1, d_sgu), sgu_ln_b[l].reshape(1, d_sgu),
                  sgu_w[l], sgu_b[l].T, w_out[l].astype(BF16), final_g.reshape(1, d),
                  seq, t=256, d_attn=d_attn, d_sgu=d_sgu)
    return x2.reshape(batch, seq, d)
```

```python
import functools
import math

import jax
import jax.numpy as jnp
from jax import lax
from jax.experimental import pallas as pl
from jax.experimental.pallas import tpu as pltpu

HEAD_DIM = 64
N_KV_HEADS = 2
WINDOW = 128
SGU_GROUPS = 8
EPS = 1e-6

F32 = jnp.float32
BF16 = jnp.bfloat16

LANES = 128
VMEM_LIMIT_BYTES = 56 * 1024 * 1024


def _silu(v):
    return v / (1.0 + jnp.exp(-v))


def _adaln_kernel(c_ref, w_ref, b_ref, o_ref):
    c_act = _silu(c_ref[...])
    acc = jnp.dot(c_act.astype(BF16), w_ref[...].astype(BF16), preferred_element_type=F32)
    o_ref[...] = acc + b_ref[...]


def _adaln(c_pad, w_ada, b_ada, tn):
    rows, d = c_pad.shape
    n = w_ada.shape[1]
    return pl.pallas_call(
        _adaln_kernel,
        grid=(n // tn,),
        in_specs=[
            pl.BlockSpec((rows, d), lambda j: (0, 0)),
            pl.BlockSpec((d, tn), lambda j: (0, j)),
            pl.BlockSpec((1, tn), lambda j: (0, j)),
        ],
        out_specs=pl.BlockSpec((rows, tn), lambda j: (0, j)),
        out_shape=jax.ShapeDtypeStruct((rows, n), F32),
        compiler_params=pltpu.CompilerParams(
            dimension_semantics=("arbitrary",), vmem_limit_bytes=VMEM_LIMIT_BYTES),
        name="adaln",
    )(c_pad, w_ada, b_ada)


def _inproj_kernel(x_ref, mod_ref, g_ref, w_ref, z_ref, h_ref, *, row_chunk):
    tm = x_ref.shape[0]

    @pl.when(pl.program_id(1) == 0)
    def _():
        shift = mod_ref[0, 0:1, :]
        gain = g_ref[...] * (1.0 + mod_ref[0, 1:2, :])
        for r in range(0, tm, row_chunk):
            x = x_ref[r:r + row_chunk, :]
            ms = jnp.mean(x * x, axis=-1, keepdims=True)
            y = x * lax.rsqrt(ms + EPS)
            h_ref[r:r + row_chunk, :] = (y * gain + shift).astype(BF16)

    z_ref[...] = jnp.dot(h_ref[...], w_ref[...], preferred_element_type=F32).astype(BF16)


def _inproj(x2, mod3, norm_g, w_in_bf, seq, tm, tn):
    m, d = x2.shape
    n = w_in_bf.shape[1]
    tiles_per_seq = seq // tm
    return pl.pallas_call(
        functools.partial(_inproj_kernel, row_chunk=128),
        grid=(m // tm, n // tn),
        in_specs=[
            pl.BlockSpec((tm, d), lambda i, j: (i, 0)),
            pl.BlockSpec((1, 3, d), lambda i, j: (i // tiles_per_seq, 0, 0)),
            pl.BlockSpec((1, d), lambda i, j: (0, 0)),
            pl.BlockSpec((d, tn), lambda i, j: (0, j)),
        ],
        out_specs=pl.BlockSpec((tm, tn), lambda i, j: (i, j)),
        out_shape=jax.ShapeDtypeStruct((m, n), BF16),
        scratch_shapes=[pltpu.VMEM((tm, d), BF16)],
        compiler_params=pltpu.CompilerParams(
            dimension_semantics=("arbitrary", "arbitrary"), vmem_limit_bytes=VMEM_LIMIT_BYTES),
        name="inproj",
    )(x2, mod3, norm_g, w_in_bf)


def _mix_kernel(sinks_ref, z_ref, kvp_ref, x_ref, mod_ref, lng_ref, lnb_ref, sw_ref, sbt_ref,
                wout_ref, fg_ref, o_ref, a_ref, *, d_attn, d_sgu):
    t = z_ref.shape[0]
    blk = WINDOW
    d_kv = N_KV_HEADS * HEAD_DIM
    pairs_per_kv = d_attn // LANES // N_KV_HEADS
    rows4 = pairs_per_kv * blk

    off_k = d_attn
    off_v = off_k + d_kv
    off_ga = off_v + d_kv
    off_u = off_ga + d_attn
    off_vs = off_u + d_sgu
    off_gs = off_vs + d_sgu

    lane2 = lax.broadcasted_iota(jnp.int32, (2 * blk, LANES), 1)
    lo2 = lane2 < HEAD_DIM
    row = lax.broadcasted_iota(jnp.int32, (blk, blk), 0)
    col = lax.broadcasted_iota(jnp.int32, (blk, blk), 1)
    tri = col <= row
    key4 = lax.broadcasted_iota(jnp.int32, (blk, rows4), 0)
    qry4 = lax.broadcasted_iota(jnp.int32, (blk, rows4), 1) & (blk - 1)
    cur4 = key4 <= qry4
    sink_row4 = key4 == 0
    keycol = lax.broadcasted_iota(jnp.int32, (HEAD_DIM, 2 * blk), 1)
    scale = 1.0 / math.sqrt(HEAD_DIM)

    first_in_seq = pl.program_id(1) == 0
    pen0 = jnp.where(first_in_seq, -jnp.inf, 0.0).astype(F32)

    for j in range(t // blk):
        r0 = j * blk
        rows = slice(r0, r0 + blk)

        k_cur = z_ref[rows, off_k:off_k + d_kv]
        v_cur = z_ref[rows, off_v:off_v + d_kv]
        if j == 0:
            k_prev = kvp_ref[:, 0:d_kv]
            v_prev = kvp_ref[:, d_kv:2 * d_kv]
        else:
            k_prev = z_ref[r0 - blk:r0, off_k:off_k + d_kv]
            v_prev = z_ref[r0 - blk:r0, off_v:off_v + d_kv]
        kext = jnp.concatenate([k_prev, k_cur], axis=0).astype(F32) * scale
        kroll = pltpu.roll(kext, HEAD_DIM, 1)
        zero2 = jnp.zeros_like(kext)
        k_lo = [jnp.where(lo2, kext, zero2), jnp.where(lo2, kroll, zero2)]
        k_hi = [jnp.where(lo2, zero2, kroll), jnp.where(lo2, zero2, kext)]
        vext_t = jnp.concatenate([v_prev, v_cur], axis=0).astype(F32).T

        for kv in range(N_KV_HEADS):
            p0 = kv * pairs_per_kv
            kbd = jnp.concatenate([k_lo[kv], k_hi[kv]], axis=0).astype(BF16)
            q4 = jnp.concatenate(
                [z_ref[rows, (p0 + p) * LANES:(p0 + p + 1) * LANES] for p in range(pairs_per_kv)],
                axis=0)
            s_t = lax.dot_general(kbd, q4, (((1,), (1,)), ((), ())),
                                  preferred_element_type=F32)
            v_t = vext_t[kv * HEAD_DIM:(kv + 1) * HEAD_DIM, :]
            v_t = jnp.where(keycol == 0, 0.0, v_t)
            v_aug = jnp.concatenate([v_t, jnp.ones((16, 2 * blk), F32)], axis=0).astype(BF16)

            pm = []
            for e in range(2):
                s_prev = s_t[2 * blk * e:2 * blk * e + blk, :]
                s_cur = s_t[2 * blk * e + blk:2 * blk * (e + 1), :]
                if j == 0:
                    s_prev = s_prev + pen0
                sc = jnp.where(cur4, s_cur, s_prev)
                sink = jnp.concatenate(
                    [jnp.full((1, blk), sinks_ref[2 * (p0 + p) + e], F32)
                     for p in range(pairs_per_kv)], axis=1)
                m = jnp.maximum(jnp.max(sc, axis=0, keepdims=True), sink)
                pr = jnp.exp(sc - m)
                p_sink = jnp.exp(sink - m)
                zero4 = jnp.zeros_like(pr)
                pm_prev = jnp.where(sink_row4, p_sink, jnp.where(cur4, zero4, pr))
                pm_cur = jnp.where(cur4, pr, zero4)
                pm.append(jnp.concatenate([pm_prev, pm_cur], axis=0).astype(BF16))

            for p in range(pairs_per_kv):
                cs = slice(p * blk, (p + 1) * blk)
                pm_pair = jnp.concatenate([pm[0][:, cs], pm[1][:, cs]], axis=1)
                o_t = jnp.dot(v_aug, pm_pair, preferred_element_type=F32)
                inv = 1.0 / o_t[HEAD_DIM:HEAD_DIM + 8, :]
                inv = jnp.concatenate([inv] * (HEAD_DIM // 8), axis=0)
                att_t = o_t[0:HEAD_DIM, :] * inv
                att = jnp.concatenate([att_t[:, 0:blk], att_t[:, blk:2 * blk]], axis=0).T
                c0 = (p0 + p) * LANES
                g = z_ref[rows, off_ga + c0:off_ga + c0 + LANES].astype(F32)
                a_ref[rows, c0:c0 + LANES] = (att * _silu(g)).astype(BF16)

        vs = z_ref[rows, off_vs:off_vs + d_sgu].astype(F32)
        mu = jnp.mean(vs, axis=-1, keepdims=True)
        dv = vs - mu
        var = jnp.mean(dv * dv, axis=-1, keepdims=True)
        vn = ((dv * lax.rsqrt(var + EPS)) * lng_ref[...] + lnb_ref[...]).astype(BF16)
        for g in range(SGU_GROUPS):
            c0 = g * LANES
            w = jnp.where(tri, sw_ref[g], 0.0).astype(BF16)
            mixed = jnp.dot(w, vn[:, c0:c0 + LANES], preferred_element_type=F32) + sbt_ref[:, g:g + 1]
            u = z_ref[rows, off_u + c0:off_u + c0 + LANES].astype(F32)
            gg = z_ref[rows, off_gs + c0:off_gs + c0 + LANES].astype(F32)
            a_ref[rows, d_attn + c0:d_attn + c0 + LANES] = ((u * mixed) * _silu(gg)).astype(BF16)

    y = jnp.dot(a_ref[...], wout_ref[...], preferred_element_type=F32)
    gate = mod_ref[0, 2:3, :]
    fg = fg_ref[...]
    for j in range(t // blk):
        rows = slice(j * blk, (j + 1) * blk)
        xn = x_ref[rows, :] + gate * y[rows, :]
        ms = jnp.mean(xn * xn, axis=-1, keepdims=True)
        o_ref[rows, :] = (xn * lax.rsqrt(ms + EPS)) * fg


def _mix(sinks, z, x2, mod3, ln_g, ln_b, sgu_w, sgu_bt, w_out_bf, final_g, seq, t, d_attn, d_sgu):
    m, d = x2.shape
    d_in = z.shape[1]
    d_mix = w_out_bf.shape[0]
    d_kv = N_KV_HEADS * HEAD_DIM
    steps = seq // t
    blocks_per_step = t // WINDOW
    blocks_per_seq = seq // WINDOW
    kv_col_block = d_attn // (2 * d_kv)
    assert d_attn % (2 * d_kv) == 0

    def prev_kv_map(b, i):
        return (b * blocks_per_seq + jnp.maximum(i * blocks_per_step - 1, 0), kv_col_block)

    return pl.pallas_call(
        functools.partial(_mix_kernel, d_attn=d_attn, d_sgu=d_sgu),
        grid=(m // seq, steps),
        in_specs=[
            pl.BlockSpec(memory_space=pltpu.SMEM),
            pl.BlockSpec((t, d_in), lambda b, i: (b * steps + i, 0)),
            pl.BlockSpec((WINDOW, 2 * d_kv), prev_kv_map),
            pl.BlockSpec((t, d), lambda b, i: (b * steps + i, 0)),
            pl.BlockSpec((1, 3, d), lambda b, i: (b, 0, 0)),
            pl.BlockSpec((1, d_sgu), lambda b, i: (0, 0)),
            pl.BlockSpec((1, d_sgu), lambda b, i: (0, 0)),
            pl.BlockSpec((SGU_GROUPS, WINDOW, WINDOW), lambda b, i: (0, 0, 0)),
            pl.BlockSpec((WINDOW, SGU_GROUPS), lambda b, i: (0, 0)),
            pl.BlockSpec((d_mix, d), lambda b, i: (0, 0)),
            pl.BlockSpec((1, d), lambda b, i: (0, 0)),
        ],
        out_specs=pl.BlockSpec((t, d), lambda b, i: (b * steps + i, 0)),
        out_shape=jax.ShapeDtypeStruct((m, d), F32),
        scratch_shapes=[pltpu.VMEM((t, d_mix), BF16)],
        compiler_params=pltpu.CompilerParams(
            dimension_semantics=("arbitrary", "arbitrary"), vmem_limit_bytes=VMEM_LIMIT_BYTES),
        name="mix",
    )(sinks, z, z, x2, mod3, ln_g, ln_b, sgu_w, sgu_bt, w_out_bf, final_g)


def kernel(x, c, norm_g, w_ada, b_ada, w_in, attn_sinks, sgu_ln_g, sgu_ln_b, sgu_w, sgu_b, w_out, final_g):
    batch, seq, d = x.shape
    depth = norm_g.shape[0]
    n_q_heads = attn_sinks.shape[1]
    d_attn = n_q_heads * HEAD_DIM
    d_sgu = sgu_ln_g.shape[1]
    assert sgu_w.shape[1:] == (SGU_GROUPS, WINDOW, WINDOW) and d_sgu == SGU_GROUPS * LANES

    c_pad = jnp.pad(c, ((0, 8 - batch % 8), (0, 0))) if batch % 8 else c
    x2 = x.reshape(batch * seq, d)
    for l in range(depth):
        mod = _adaln(c_pad, w_ada[l], b_ada[l].reshape(1, -1), tn=512)[:batch]
        mod3 = mod.reshape(batch, 3, d)
        z = _inproj(x2, mod3, norm_g[l].reshape(1, d), w_in[l].astype(BF16), seq, tm=1024, tn=768)
        last = l == depth - 1
        assert last, "stacked layers need the un-normalised residual stream between layers"
        x2 = _mix(attn_sinks[l], z, x2, mod3, sgu_ln_g[l].reshape(1, d_sgu), sgu_ln_b[l].reshape(1, d_sgu),
                  sgu_w[l], sgu_b[l].T, w_out[l].astype(BF16), final_g.reshape(1, d),
                  seq, t=256, d_attn=d_attn, d_sgu=d_sgu)
    return x2.reshape(batch, seq, d)
```

```python
import functools
import math

import jax
import jax.numpy as jnp
from jax import lax
from jax.experimental import pallas as pl
from jax.experimental.pallas import tpu as pltpu

HEAD_DIM = 64
N_KV_HEADS = 2
WINDOW = 128
SGU_GROUPS = 8
EPS = 1e-6

F32 = jnp.float32
BF16 = jnp.bfloat16

LANES = 128
MXU_WIDTH = 256
OUT_PANELS = 2
VMEM_LIMIT_BYTES = 56 * 1024 * 1024


def _silu(v):
    return v / (1.0 + jnp.exp(-v))


def _adaln_kernel(c_ref, w_ref, b_ref, o_ref):
    c_act = _silu(c_ref[...])
    acc = jnp.dot(c_act.astype(BF16), w_ref[...].astype(BF16), preferred_element_type=F32)
    o_ref[...] = acc + b_ref[...]


def _adaln(c_pad, w_ada, b_ada, tn):
    rows, d = c_pad.shape
    n = w_ada.shape[1]
    return pl.pallas_call(
        _adaln_kernel,
        grid=(n // tn,),
        in_specs=[
            pl.BlockSpec((rows, d), lambda j: (0, 0)),
            pl.BlockSpec((d, tn), lambda j: (0, j)),
            pl.BlockSpec((1, tn), lambda j: (0, j)),
        ],
        out_specs=pl.BlockSpec((rows, tn), lambda j: (0, j)),
        out_shape=jax.ShapeDtypeStruct((rows, n), F32),
        compiler_params=pltpu.CompilerParams(
            dimension_semantics=("arbitrary",), vmem_limit_bytes=VMEM_LIMIT_BYTES),
        name="adaln",
    )(c_pad, w_ada, b_ada)


def _inproj_kernel(x_ref, mod_ref, g_ref, w_ref, z_ref, h_ref, *, row_chunk):
    tm = x_ref.shape[0]

    @pl.when(pl.program_id(1) == 0)
    def _():
        shift = mod_ref[0, 0:1, :]
        gain = g_ref[...] * (1.0 + mod_ref[0, 1:2, :])
        for r in range(0, tm, row_chunk):
            x = x_ref[r:r + row_chunk, :]
            ms = jnp.mean(x * x, axis=-1, keepdims=True)
            y = x * lax.rsqrt(ms + EPS)
            h_ref[r:r + row_chunk, :] = (y * gain + shift).astype(BF16)

    z_ref[...] = jnp.dot(h_ref[...], w_ref[...], preferred_element_type=F32).astype(BF16)


def _inproj(x2, mod3, norm_g, w_in_bf, seq, tm, tn):
    m, d = x2.shape
    n = w_in_bf.shape[1]
    tiles_per_seq = seq // tm
    return pl.pallas_call(
        functools.partial(_inproj_kernel, row_chunk=128),
        grid=(m // tm, n // tn),
        in_specs=[
            pl.BlockSpec((tm, d), lambda i, j: (i, 0)),
            pl.BlockSpec((1, 3, d), lambda i, j: (i // tiles_per_seq, 0, 0)),
            pl.BlockSpec((1, d), lambda i, j: (0, 0)),
            pl.BlockSpec((d, tn), lambda i, j: (0, j)),
        ],
        out_specs=pl.BlockSpec((tm, tn), lambda i, j: (i, j)),
        out_shape=jax.ShapeDtypeStruct((m, n), BF16),
        scratch_shapes=[pltpu.VMEM((tm, d), BF16)],
        compiler_params=pltpu.CompilerParams(
            dimension_semantics=("arbitrary", "arbitrary"), vmem_limit_bytes=VMEM_LIMIT_BYTES),
        name="inproj",
    )(x2, mod3, norm_g, w_in_bf)


def _mix_kernel(sinks_ref, z_ref, kvp_ref, x_ref, mod_ref, lng_ref, lnb_ref, sw_ref, sbt_ref,
                wout_ref, fg_ref, o_ref, a_cur_ref, a_prev_ref, y_ref, *, d_attn, d_sgu, steps_per_seq, n_tiles):
    t = z_ref.shape[0]
    d_model = o_ref.shape[1]
    s_id = pl.program_id(0)
    blk = WINDOW
    n_blk = t // blk
    d_kv = N_KV_HEADS * HEAD_DIM
    pairs_per_kv = d_attn // LANES // N_KV_HEADS
    rows4 = pairs_per_kv * blk

    off_k = d_attn
    off_v = off_k + d_kv
    off_ga = off_v + d_kv
    off_u = off_ga + d_attn
    off_vs = off_u + d_sgu
    off_gs = off_vs + d_sgu

    @pl.when(s_id == 0)
    def _():
        a_prev_ref[...] = jnp.zeros_like(a_prev_ref)

    lane2 = lax.broadcasted_iota(jnp.int32, (2 * blk, LANES), 1)
    lo2 = lane2 < HEAD_DIM
    row = lax.broadcasted_iota(jnp.int32, (blk, blk), 0)
    col = lax.broadcasted_iota(jnp.int32, (blk, blk), 1)
    tri = col <= row
    key4 = lax.broadcasted_iota(jnp.int32, (blk, rows4), 0)
    qry4 = lax.broadcasted_iota(jnp.int32, (blk, rows4), 1) & (blk - 1)
    cur4 = key4 <= qry4
    sink_row4 = key4 == 0
    keycol = lax.broadcasted_iota(jnp.int32, (HEAD_DIM, 2 * blk), 1)
    scale = 1.0 / math.sqrt(HEAD_DIM)

    first_in_seq = lax.rem(jnp.minimum(s_id, n_tiles - 1), steps_per_seq) == 0
    pen0 = jnp.where(first_in_seq, -jnp.inf, 0.0).astype(F32)

    def out_chunk(k):
        for n in range(k * OUT_PANELS, (k + 1) * OUT_PANELS):
            y_ref[:, n * MXU_WIDTH:(n + 1) * MXU_WIDTH] = jnp.dot(
                a_prev_ref[...], wout_ref[n], preferred_element_type=F32)

    def finish(j):
        rows = slice(j * blk, (j + 1) * blk)
        xn = x_ref[rows, :] + mod_ref[0, 2:3, :] * y_ref[rows, :]
        ms = jnp.mean(xn * xn, axis=-1, keepdims=True)
        o_ref[rows, :] = (xn * lax.rsqrt(ms + EPS)) * fg_ref[...]

    kv_cache = {}

    def block_kv(j):
        if j in kv_cache:
            return kv_cache[j]
        r0 = j * blk
        rows = slice(r0, r0 + blk)
        k_cur = z_ref[rows, off_k:off_k + d_kv]
        v_cur = z_ref[rows, off_v:off_v + d_kv]
        if j == 0:
            k_prev = kvp_ref[:, 0:d_kv]
            v_prev = kvp_ref[:, d_kv:2 * d_kv]
        else:
            k_prev = z_ref[r0 - blk:r0, off_k:off_k + d_kv]
            v_prev = z_ref[r0 - blk:r0, off_v:off_v + d_kv]
        kext = jnp.concatenate([k_prev, k_cur], axis=0).astype(F32) * scale
        kroll = pltpu.roll(kext, HEAD_DIM, 1)
        zero2 = jnp.zeros_like(kext)
        k_lo = [jnp.where(lo2, kext, zero2), jnp.where(lo2, kroll, zero2)]
        k_hi = [jnp.where(lo2, zero2, kroll), jnp.where(lo2, zero2, kext)]
        vext_t = jnp.concatenate([v_prev, v_cur], axis=0).astype(F32).T
        kv_cache[j] = (k_lo, k_hi, vext_t)
        return kv_cache[j]

    def attention(j, kv):
        rows = slice(j * blk, (j + 1) * blk)
        k_lo, k_hi, vext_t = block_kv(j)
        p0 = kv * pairs_per_kv
        kbd = jnp.concatenate([k_lo[kv], k_hi[kv]], axis=0).astype(BF16)
        q4 = jnp.concatenate(
            [z_ref[rows, (p0 + p) * LANES:(p0 + p + 1) * LANES] for p in range(pairs_per_kv)],
            axis=0)
        s_t = lax.dot_general(kbd, q4, (((1,), (1,)), ((), ())),
                              preferred_element_type=F32)
        yield
        v_t = vext_t[kv * HEAD_DIM:(kv + 1) * HEAD_DIM, :]
        v_t = jnp.where(keycol == 0, 0.0, v_t)
        v_aug = jnp.concatenate([v_t, jnp.ones((16, 2 * blk), F32)], axis=0).astype(BF16)

        pm = []
        for e in range(2):
            s_prev = s_t[2 * blk * e:2 * blk * e + blk, :]
            s_cur = s_t[2 * blk * e + blk:2 * blk * (e + 1), :]
            if j == 0:
                s_prev = s_prev + pen0
            sc = jnp.where(cur4, s_cur, s_prev)
            sink = jnp.concatenate(
                [jnp.full((1, blk), sinks_ref[2 * (p0 + p) + e], F32)
                 for p in range(pairs_per_kv)], axis=1)
            m = jnp.maximum(jnp.max(sc, axis=0, keepdims=True), sink)
            pr = jnp.exp(sc - m)
            p_sink = jnp.exp(sink - m)
            zero4 = jnp.zeros_like(pr)
            pm_prev = jnp.where(sink_row4, p_sink, jnp.where(cur4, zero4, pr))
            pm_cur = jnp.where(cur4, pr, zero4)
            pm.append(jnp.concatenate([pm_prev, pm_cur], axis=0).astype(BF16))

        for p in range(pairs_per_kv):
            cs = slice(p * blk, (p + 1) * blk)
            pm_pair = jnp.concatenate([pm[0][:, cs], pm[1][:, cs]], axis=1)
            o_t = jnp.dot(v_aug, pm_pair, preferred_element_type=F32)
            inv = 1.0 / o_t[HEAD_DIM:HEAD_DIM + 8, :]
            inv = jnp.concatenate([inv] * (HEAD_DIM // 8), axis=0)
            att_t = o_t[0:HEAD_DIM, :] * inv
            att = jnp.concatenate([att_t[:, 0:blk], att_t[:, blk:2 * blk]], axis=0).T
            c0 = (p0 + p) * LANES
            g = z_ref[rows, off_ga + c0:off_ga + c0 + LANES].astype(F32)
            a_cur_ref[rows, c0:c0 + LANES] = (att * _silu(g)).astype(BF16)

    def spatial_gating(j):
        rows = slice(j * blk, (j + 1) * blk)
        vs = z_ref[rows, off_vs:off_vs + d_sgu].astype(F32)
        mu = jnp.mean(vs, axis=-1, keepdims=True)
        dv = vs - mu
        var = jnp.mean(dv * dv, axis=-1, keepdims=True)
        vn = ((dv * lax.rsqrt(var + EPS)) * lng_ref[...] + lnb_ref[...]).astype(BF16)
        yield
        for g in range(SGU_GROUPS):
            c0 = g * LANES
            w = jnp.where(tri, sw_ref[g], 0.0).astype(BF16)
            mixed = jnp.dot(w, vn[:, c0:c0 + LANES], preferred_element_type=F32) + sbt_ref[:, g:g + 1]
            u = z_ref[rows, off_u + c0:off_u + c0 + LANES].astype(F32)
            gg = z_ref[rows, off_gs + c0:off_gs + c0 + LANES].astype(F32)
            a_cur_ref[rows, d_attn + c0:d_attn + c0 + LANES] = ((u * mixed) * _silu(gg)).astype(BF16)

    mix_pieces = []
    for j in range(n_blk):
        mix_pieces += [attention(j, kv) for kv in range(N_KV_HEADS)]
        mix_pieces.append(spatial_gating(j))
    out_pieces = [functools.partial(out_chunk, k) for k in range(d_model // (MXU_WIDTH * OUT_PANELS))]
    for piece in mix_pieces:
        next(piece)
        if out_pieces:
            out_pieces.pop(0)()
        for _ in piece:
            pass
    for out_piece in out_pieces:
        out_piece()
    for j in range(n_blk):
        finish(j)

    a_prev_ref[...] = a_cur_ref[...]


def _column_panels(w):
    k, n = w.shape
    return w.reshape(k, n // MXU_WIDTH, MXU_WIDTH).transpose(1, 0, 2)


def _mix(sinks, z, x2, mod3, ln_g, ln_b, sgu_w, sgu_bt, w_out_bf, final_g, seq, t, d_attn, d_sgu):
    m, d = x2.shape
    d_in = z.shape[1]
    d_mix = w_out_bf.shape[1]
    d_kv = N_KV_HEADS * HEAD_DIM
    steps = seq // t
    n_tiles = m // t
    blocks_per_step = t // WINDOW
    kv_col_block = d_attn // (2 * d_kv)
    assert d_attn % (2 * d_kv) == 0

    def mix_tile(s):
        return jnp.minimum(s, n_tiles - 1)

    def out_tile(s):
        return jnp.maximum(s - 1, 0)

    def prev_kv_map(s):
        return (jnp.maximum(mix_tile(s) * blocks_per_step - 1, 0), kv_col_block)

    const2 = lambda s: (0, 0)
    return pl.pallas_call(
        functools.partial(_mix_kernel, d_attn=d_attn, d_sgu=d_sgu, steps_per_seq=steps, n_tiles=n_tiles),
        grid=(n_tiles + 1,),
        in_specs=[
            pl.BlockSpec(memory_space=pltpu.SMEM),
            pl.BlockSpec((t, d_in), lambda s: (mix_tile(s), 0)),
            pl.BlockSpec((WINDOW, 2 * d_kv), prev_kv_map),
            pl.BlockSpec((t, d), lambda s: (out_tile(s), 0)),
            pl.BlockSpec((1, 3, d), lambda s: (out_tile(s) // steps, 0, 0)),
            pl.BlockSpec((1, d_sgu), const2),
            pl.BlockSpec((1, d_sgu), const2),
            pl.BlockSpec((SGU_GROUPS, WINDOW, WINDOW), lambda s: (0, 0, 0)),
            pl.BlockSpec((WINDOW, SGU_GROUPS), const2),
            pl.BlockSpec(w_out_bf.shape, lambda s: (0, 0, 0)),
            pl.BlockSpec((1, d), const2),
        ],
        out_specs=pl.BlockSpec((t, d), lambda s: (out_tile(s), 0)),
        out_shape=jax.ShapeDtypeStruct((m, d), F32),
        scratch_shapes=[pltpu.VMEM((t, d_mix), BF16), pltpu.VMEM((t, d_mix), BF16), pltpu.VMEM((t, d), F32)],
        compiler_params=pltpu.CompilerParams(
            dimension_semantics=("arbitrary",), vmem_limit_bytes=VMEM_LIMIT_BYTES),
        name="mix",
    )(sinks, z, z, x2, mod3, ln_g, ln_b, sgu_w, sgu_bt, w_out_bf, final_g)


def kernel(x, c, norm_g, w_ada, b_ada, w_in, attn_sinks, sgu_ln_g, sgu_ln_b, sgu_w, sgu_b, w_out, final_g):
    batch, seq, d = x.shape
    depth = norm_g.shape[0]
    n_q_heads = attn_sinks.shape[1]
    d_attn = n_q_heads * HEAD_DIM
    d_sgu = sgu_ln_g.shape[1]
    assert sgu_w.shape[1:] == (SGU_GROUPS, WINDOW, WINDOW) and d_sgu == SGU_GROUPS * LANES

    c_pad = jnp.pad(c, ((0, 8 - batch % 8), (0, 0))) if batch % 8 else c
    x2 = x.reshape(batch * seq, d)
    for l in range(depth):
        mod = _adaln(c_pad, w_ada[l], b_ada[l].reshape(1, -1), tn=512)[:batch]
        mod3 = mod.reshape(batch, 3, d)
        z = _inproj(x2, mod3, norm_g[l].reshape(1, d), w_in[l].astype(BF16), seq, tm=1024, tn=768)
        last = l == depth - 1
        assert last, "stacked layers need the un-normalised residual stream between layers"
        x2 = _mix(attn_sinks[l], z, x2, mod3, sgu_ln_g[l].reshape(1, d_sgu), sgu_ln_b[l].reshape(1, d_sgu),
                  sgu_w[l], sgu_b[l].T, _column_panels(w_out[l].astype(BF16)), final_g.reshape(1, d),
                  seq, t=256, d_attn=d_attn, d_sgu=d_sgu)
    return x2.reshape(batch, seq, d)
```

```python
import functools
import math

import jax
import jax.numpy as jnp
from jax import lax
from jax.experimental import pallas as pl
from jax.experimental.pallas import tpu as pltpu

HEAD_DIM = 64
N_KV_HEADS = 2
WINDOW = 128
SGU_GROUPS = 8
EPS = 1e-6

F32 = jnp.float32
BF16 = jnp.bfloat16

LANES = 128
MXU_WIDTH = 256
VMEM_LIMIT_BYTES = 60 * 1024 * 1024
TILE_TOKENS = 256


def _silu(v):
    return v / (1.0 + jnp.exp(-v))


def _adaln_kernel(c_ref, w_ref, b_ref, o_ref):
    c_act = _silu(c_ref[...])
    acc = jnp.dot(c_act.astype(BF16), w_ref[...].astype(BF16), preferred_element_type=F32)
    o_ref[...] = acc + b_ref[...]


def _adaln(c_pad, w_ada, b_ada, tn):
    rows, d = c_pad.shape
    n = w_ada.shape[1]
    return pl.pallas_call(
        _adaln_kernel,
        grid=(n // tn,),
        in_specs=[
            pl.BlockSpec((rows, d), lambda j: (0, 0)),
            pl.BlockSpec((d, tn), lambda j: (0, j)),
            pl.BlockSpec((1, tn), lambda j: (0, j)),
        ],
        out_specs=pl.BlockSpec((rows, tn), lambda j: (0, j)),
        out_shape=jax.ShapeDtypeStruct((rows, n), F32),
        compiler_params=pltpu.CompilerParams(
            dimension_semantics=("arbitrary",), vmem_limit_bytes=VMEM_LIMIT_BYTES),
        name="adaln",
    )(c_pad, w_ada, b_ada)


def _layer_kernel(sinks_ref, x_ref, mod_ref, ng_ref, win_ref, lng_ref, lnb_ref, sw_ref, sbt_ref,
                  wout_ref, fg_ref, o_ref,
                  h_ref, z_ref, kvp_ref, a_cur_ref, a_prev_ref, y_ref, xp_ref,
                  *, d_attn, d_sgu, steps_per_seq, n_tiles):
    t, d_model = x_ref.shape
    s_id = pl.program_id(0)
    blk = WINDOW
    n_blk = t // blk
    d_kv = N_KV_HEADS * HEAD_DIM
    pairs_per_kv = d_attn // LANES // N_KV_HEADS
    rows4 = pairs_per_kv * blk

    off_k = d_attn
    off_v = off_k + d_kv
    off_ga = off_v + d_kv
    off_u = off_ga + d_attn
    off_vs = off_u + d_sgu
    off_gs = off_vs + d_sgu
    d_in = off_gs + d_sgu

    tile_in = jnp.minimum(s_id, n_tiles - 1)
    b_in = tile_in // steps_per_seq
    b_out = jnp.maximum(s_id - 1, 0) // steps_per_seq
    first_in_seq = lax.rem(tile_in, steps_per_seq) == 0
    pen0 = jnp.where(first_in_seq, -jnp.inf, 0.0).astype(F32)

    @pl.when(s_id == 0)
    def _():
        a_prev_ref[...] = jnp.zeros_like(a_prev_ref)
        xp_ref[...] = jnp.zeros_like(xp_ref)
        z_ref[t - blk:t, off_k:off_k + 2 * d_kv] = jnp.zeros((blk, 2 * d_kv), BF16)

    kvp_ref[...] = z_ref[t - blk:t, off_k:off_k + 2 * d_kv]

    lane2 = lax.broadcasted_iota(jnp.int32, (2 * blk, LANES), 1)
    lo2 = lane2 < HEAD_DIM
    row = lax.broadcasted_iota(jnp.int32, (blk, blk), 0)
    col = lax.broadcasted_iota(jnp.int32, (blk, blk), 1)
    tri = col <= row
    key4 = lax.broadcasted_iota(jnp.int32, (blk, rows4), 0)
    qry4 = lax.broadcasted_iota(jnp.int32, (blk, rows4), 1) & (blk - 1)
    cur4 = key4 <= qry4
    sink_row4 = key4 == 0
    keycol = lax.broadcasted_iota(jnp.int32, (HEAD_DIM, 2 * blk), 1)
    scale = 1.0 / math.sqrt(HEAD_DIM)

    def modulated_norm():
        shift = mod_ref[b_in, 0:1, :]
        gain = ng_ref[...] * (1.0 + mod_ref[b_in, 1:2, :])
        for r in range(0, t, blk):
            x = x_ref[r:r + blk, :]
            ms = jnp.mean(x * x, axis=-1, keepdims=True)
            h_ref[r:r + blk, :] = ((x * lax.rsqrt(ms + EPS)) * gain + shift).astype(BF16)

    def in_panels(first, count):
        for n in range(first, first + count):
            cs = slice(n * MXU_WIDTH, (n + 1) * MXU_WIDTH)
            z_ref[:, cs] = jnp.dot(h_ref[...], win_ref[:, cs], preferred_element_type=F32).astype(BF16)

    def out_panels(first, count):
        for n in range(first, first + count):
            cs = slice(n * MXU_WIDTH, (n + 1) * MXU_WIDTH)
            y_ref[:, cs] = jnp.dot(a_prev_ref[...], wout_ref[:, cs], preferred_element_type=F32)

    def finish(j):
        rows = slice(j * blk, (j + 1) * blk)
        xn = xp_ref[rows, :] + mod_ref[b_out, 2:3, :] * y_ref[rows, :]
        ms = jnp.mean(xn * xn, axis=-1, keepdims=True)
        o_ref[rows, :] = (xn * lax.rsqrt(ms + EPS)) * fg_ref[...]

    kv_cache = {}

    def block_kv(j):
        if j in kv_cache:
            return kv_cache[j]
        r0 = j * blk
        rows = slice(r0, r0 + blk)
        k_cur = z_ref[rows, off_k:off_k + d_kv]
        v_cur = z_ref[rows, off_v:off_v + d_kv]
        if j == 0:
            k_prev = kvp_ref[:, 0:d_kv]
            v_prev = kvp_ref[:, d_kv:2 * d_kv]
        else:
            k_prev = z_ref[r0 - blk:r0, off_k:off_k + d_kv]
            v_prev = z_ref[r0 - blk:r0, off_v:off_v + d_kv]
        kext = jnp.concatenate([k_prev, k_cur], axis=0).astype(F32) * scale
        kroll = pltpu.roll(kext, HEAD_DIM, 1)
        zero2 = jnp.zeros_like(kext)
        k_lo = [jnp.where(lo2, kext, zero2), jnp.where(lo2, kroll, zero2)]
        k_hi = [jnp.where(lo2, zero2, kroll), jnp.where(lo2, zero2, kext)]
        vext_t = jnp.concatenate([v_prev, v_cur], axis=0).astype(F32).T
        kv_cache[j] = (k_lo, k_hi, vext_t)
        return kv_cache[j]

    def attention(j, kv):
        rows = slice(j * blk, (j + 1) * blk)
        k_lo, k_hi, vext_t = block_kv(j)
        p0 = kv * pairs_per_kv
        kbd = jnp.concatenate([k_lo[kv], k_hi[kv]], axis=0).astype(BF16)
        q4 = jnp.concatenate(
            [z_ref[rows, (p0 + p) * LANES:(p0 + p + 1) * LANES] for p in range(pairs_per_kv)],
            axis=0)
        s_t = lax.dot_general(kbd, q4, (((1,), (1,)), ((), ())),
                              preferred_element_type=F32)
        yield
        v_t = vext_t[kv * HEAD_DIM:(kv + 1) * HEAD_DIM, :]
        v_t = jnp.where(keycol == 0, 0.0, v_t)
        v_aug = jnp.concatenate([v_t, jnp.ones((16, 2 * blk), F32)], axis=0).astype(BF16)

        pm = []
        for e in range(2):
            s_prev = s_t[2 * blk * e:2 * blk * e + blk, :]
            s_cur = s_t[2 * blk * e + blk:2 * blk * (e + 1), :]
            if j == 0:
                s_prev = s_prev + pen0
            sc = jnp.where(cur4, s_cur, s_prev)
            sink = jnp.concatenate(
                [jnp.full((1, blk), sinks_ref[2 * (p0 + p) + e], F32)
                 for p in range(pairs_per_kv)], axis=1)
            m = jnp.maximum(jnp.max(sc, axis=0, keepdims=True), sink)
            pr = jnp.exp(sc - m)
            p_sink = jnp.exp(sink - m)
            zero4 = jnp.zeros_like(pr)
            pm_prev = jnp.where(sink_row4, p_sink, jnp.where(cur4, zero4, pr))
            pm_cur = jnp.where(cur4, pr, zero4)
            pm.append(jnp.concatenate([pm_prev, pm_cur], axis=0).astype(BF16))

        att = []
        for p in range(pairs_per_kv):
            cs = slice(p * blk, (p + 1) * blk)
            pm_pair = jnp.concatenate([pm[0][:, cs], pm[1][:, cs]], axis=1)
            o_t = jnp.dot(v_aug, pm_pair, preferred_element_type=F32)
            inv = 1.0 / o_t[HEAD_DIM:HEAD_DIM + 8, :]
            inv = jnp.concatenate([inv] * (HEAD_DIM // 8), axis=0)
            att_t = o_t[0:HEAD_DIM, :] * inv
            att.append(jnp.concatenate([att_t[:, 0:blk], att_t[:, blk:2 * blk]], axis=0).T)
        yield
        for p in range(pairs_per_kv):
            c0 = (p0 + p) * LANES
            g = z_ref[rows, off_ga + c0:off_ga + c0 + LANES].astype(F32)
            a_cur_ref[rows, c0:c0 + LANES] = (att[p] * _silu(g)).astype(BF16)

    def spatial_gating(j):
        rows = slice(j * blk, (j + 1) * blk)
        vs = z_ref[rows, off_vs:off_vs + d_sgu].astype(F32)
        mu = jnp.mean(vs, axis=-1, keepdims=True)
        dv = vs - mu
        var = jnp.mean(dv * dv, axis=-1, keepdims=True)
        vn = ((dv * lax.rsqrt(var + EPS)) * lng_ref[...] + lnb_ref[...]).astype(BF16)
        yield
        mixed = []
        for g in range(SGU_GROUPS):
            c0 = g * LANES
            w = jnp.where(tri, sw_ref[g], 0.0).astype(BF16)
            mixed.append(jnp.dot(w, vn[:, c0:c0 + LANES], preferred_element_type=F32) + sbt_ref[:, g:g + 1])
        yield
        for g in range(SGU_GROUPS):
            c0 = g * LANES
            u = z_ref[rows, off_u + c0:off_u + c0 + LANES].astype(F32)
            gg = z_ref[rows, off_gs + c0:off_gs + c0 + LANES].astype(F32)
            a_cur_ref[rows, d_attn + c0:d_attn + c0 + LANES] = ((u * mixed[g]) * _silu(gg)).astype(BF16)

    assert n_blk == 2 and N_KV_HEADS == 2, "the emission schedule below is written for two blocks per tile"
    panel = {name: (off // MXU_WIDTH, width // MXU_WIDTH) for name, off, width in (
        ("q", 0, d_attn), ("kv", off_k, 2 * d_kv), ("ga", off_ga, d_attn),
        ("u", off_u, d_sgu), ("vs", off_vs, d_sgu), ("gs", off_gs, d_sgu))}
    assert all(c >= 1 for _, c in panel.values()) and d_in % MXU_WIDTH == 0

    def half(name, which):
        first, count = panel[name]
        h0 = count // 2
        return (first, h0) if which == 0 else (first + h0, count - h0)

    modulated_norm()
    out_panels(0, d_model // MXU_WIDTH)
    in_panels(*panel["q"])
    in_panels(*panel["kv"])
    for j in range(n_blk):
        finish(j)
    att = [attention(j, kv) for j in range(n_blk) for kv in range(N_KV_HEADS)]
    sgu = [spatial_gating(j) for j in range(n_blk)]
    fillers = [half("vs", 0), half("vs", 1), half("ga", 0), half("ga", 1)]
    for piece, filler in zip(att, fillers):
        next(piece)
        in_panels(*filler)
        next(piece)
    for piece in sgu:
        next(piece)
    in_panels(*panel["u"])
    for piece in att:
        for _ in piece:
            pass
    for piece in sgu:
        next(piece)
    in_panels(*panel["gs"])
    for piece in sgu:
        for _ in piece:
            pass

    a_prev_ref[...] = a_cur_ref[...]
    xp_ref[...] = x_ref[...]


def _layer(sinks, x2, mod3, norm_g, w_in_bf, ln_g, ln_b, sgu_w, sgu_bt, w_out_bf, final_g, seq, d_attn, d_sgu):
    m, d = x2.shape
    t = TILE_TOKENS
    d_in = w_in_bf.shape[1]
    d_mix = w_out_bf.shape[0]
    d_kv = N_KV_HEADS * HEAD_DIM
    n_tiles = m // t
    resident = pl.Buffered(1)

    def in_tile(s):
        return jnp.minimum(s, n_tiles - 1)

    def out_tile(s):
        return jnp.maximum(s - 1, 0)

    const2 = lambda s: (0, 0)
    const3 = lambda s: (0, 0, 0)
    return pl.pallas_call(
        functools.partial(_layer_kernel, d_attn=d_attn, d_sgu=d_sgu, steps_per_seq=seq // t, n_tiles=n_tiles),
        grid=(n_tiles + 1,),
        in_specs=[
            pl.BlockSpec(memory_space=pltpu.SMEM),
            pl.BlockSpec((t, d), lambda s: (in_tile(s), 0)),
            pl.BlockSpec(mod3.shape, const3),
            pl.BlockSpec((1, d), const2),
            pl.BlockSpec((d, d_in), const2, pipeline_mode=resident),
            pl.BlockSpec((1, d_sgu), const2),
            pl.BlockSpec((1, d_sgu), const2),
            pl.BlockSpec((SGU_GROUPS, WINDOW, WINDOW), const3),
            pl.BlockSpec((WINDOW, SGU_GROUPS), const2),
            pl.BlockSpec((d_mix, d), const2, pipeline_mode=resident),
            pl.BlockSpec((1, d), const2),
        ],
        out_specs=pl.BlockSpec((t, d), lambda s: (out_tile(s), 0)),
        out_shape=jax.ShapeDtypeStruct((m, d), F32),
        scratch_shapes=[
            pltpu.VMEM((t, d), BF16),
            pltpu.VMEM((t, d_in), BF16),
            pltpu.VMEM((WINDOW, 2 * d_kv), BF16),
            pltpu.VMEM((t, d_mix), BF16),
            pltpu.VMEM((t, d_mix), BF16),
            pltpu.VMEM((t, d), F32),
            pltpu.VMEM((t, d), F32),
        ],
        compiler_params=pltpu.CompilerParams(
            dimension_semantics=("arbitrary",), vmem_limit_bytes=VMEM_LIMIT_BYTES),
        name="layer",
    )(sinks, x2, mod3, norm_g, w_in_bf, ln_g, ln_b, sgu_w, sgu_bt, w_out_bf, final_g)


def kernel(x, c, norm_g, w_ada, b_ada, w_in, attn_sinks, sgu_ln_g, sgu_ln_b, sgu_w, sgu_b, w_out, final_g):
    batch, seq, d = x.shape
    depth = norm_g.shape[0]
    n_q_heads = attn_sinks.shape[1]
    d_attn = n_q_heads * HEAD_DIM
    d_sgu = sgu_ln_g.shape[1]
    assert sgu_w.shape[1:] == (SGU_GROUPS, WINDOW, WINDOW) and d_sgu == SGU_GROUPS * LANES
    assert depth == 1, "stacked layers need the un-normalised residual stream between layers"

    c_pad = jnp.pad(c, ((0, 8 - batch % 8), (0, 0))) if batch % 8 else c
    x2 = x.reshape(batch * seq, d)
    mod = _adaln(c_pad, w_ada[0], b_ada[0].reshape(1, -1), tn=512)[:batch]
    out = _layer(attn_sinks[0], x2, mod.reshape(batch, 3, d), norm_g[0].reshape(1, d), w_in[0].astype(BF16),
                 sgu_ln_g[0].reshape(1, d_sgu), sgu_ln_b[0].reshape(1, d_sgu), sgu_w[0], sgu_b[0].T,
                 w_out[0].astype(BF16), final_g.reshape(1, d), seq, d_attn, d_sgu)
    return out.reshape(batch, seq, d)
```

```python
import functools
import math

import jax
import jax.numpy as jnp
from jax import lax
from jax.experimental import pallas as pl
from jax.experimental.pallas import tpu as pltpu

HEAD_DIM = 64
N_KV_HEADS = 2
WINDOW = 128
SGU_GROUPS = 8
EPS = 1e-6

F32 = jnp.float32
BF16 = jnp.bfloat16

LANES = 128
MXU_WIDTH = 256
VMEM_LIMIT_BYTES = 60 * 1024 * 1024
TILE_TOKENS = 256


def _silu(v):
    return v / (1.0 + jnp.exp(-v))


def _adaln_kernel(c_ref, w_ref, b_ref, o_ref):
    c_act = _silu(c_ref[...])
    acc = jnp.dot(c_act.astype(BF16), w_ref[...].astype(BF16), preferred_element_type=F32)
    o_ref[...] = acc + b_ref[...]


def _adaln(c_pad, w_ada, b_ada, tn):
    rows, d = c_pad.shape
    n = w_ada.shape[1]
    return pl.pallas_call(
        _adaln_kernel,
        grid=(n // tn,),
        in_specs=[
            pl.BlockSpec((rows, d), lambda j: (0, 0)),
            pl.BlockSpec((d, tn), lambda j: (0, j)),
            pl.BlockSpec((1, tn), lambda j: (0, j)),
        ],
        out_specs=pl.BlockSpec((rows, tn), lambda j: (0, j)),
        out_shape=jax.ShapeDtypeStruct((rows, n), F32),
        compiler_params=pltpu.CompilerParams(
            dimension_semantics=("arbitrary",), vmem_limit_bytes=VMEM_LIMIT_BYTES),
        name="adaln",
    )(c_pad, w_ada, b_ada)


def _layer_kernel(sinks_ref, x_ref, mod_ref, ng_ref, win_ref, lng_ref, lnb_ref, sw_ref, sbt_ref,
                  wout_ref, fg_ref, o_ref,
                  h_ref, z_ref, kvp_ref, a_cur_ref, a_prev_ref, y_ref, xp_ref,
                  *, d_attn, d_sgu, steps_per_seq, n_tiles):
    t, d_model = x_ref.shape
    s_id = pl.program_id(0)
    blk = WINDOW
    n_blk = t // blk
    d_kv = N_KV_HEADS * HEAD_DIM
    pairs_per_kv = d_attn // LANES // N_KV_HEADS
    rows4 = pairs_per_kv * blk

    off_k = d_attn
    off_v = off_k + d_kv
    off_ga = off_v + d_kv
    off_u = off_ga + d_attn
    off_vs = off_u + d_sgu
    off_gs = off_vs + d_sgu
    d_in = off_gs + d_sgu

    tile_in = jnp.minimum(s_id, n_tiles - 1)
    b_in = tile_in // steps_per_seq
    b_out = jnp.maximum(s_id - 1, 0) // steps_per_seq
    first_in_seq = lax.rem(tile_in, steps_per_seq) == 0
    pen0 = jnp.where(first_in_seq, -jnp.inf, 0.0).astype(F32)

    @pl.when(s_id == 0)
    def _():
        a_prev_ref[...] = jnp.zeros_like(a_prev_ref)
        xp_ref[...] = jnp.zeros_like(xp_ref)
        z_ref[t - blk:t, off_k:off_k + 2 * d_kv] = jnp.zeros((blk, 2 * d_kv), BF16)

    kvp_ref[...] = z_ref[t - blk:t, off_k:off_k + 2 * d_kv]

    lane2 = lax.broadcasted_iota(jnp.int32, (2 * blk, LANES), 1)
    lo2 = lane2 < HEAD_DIM
    row = lax.broadcasted_iota(jnp.int32, (blk, blk), 0)
    col = lax.broadcasted_iota(jnp.int32, (blk, blk), 1)
    tri = col <= row
    key4 = lax.broadcasted_iota(jnp.int32, (blk, rows4), 0)
    qry4 = lax.broadcasted_iota(jnp.int32, (blk, rows4), 1) & (blk - 1)
    cur4 = key4 <= qry4
    sink_row4 = key4 == 0
    keycol = lax.broadcasted_iota(jnp.int32, (HEAD_DIM, 2 * blk), 1)
    scale = 1.0 / math.sqrt(HEAD_DIM)

    def modulated_norm():
        shift = mod_ref[b_in, 0:1, :]
        gain = ng_ref[...] * (1.0 + mod_ref[b_in, 1:2, :])
        for r in range(0, t, blk):
            x = x_ref[r:r + blk, :]
            ms = jnp.mean(x * x, axis=-1, keepdims=True)
            h_ref[r:r + blk, :] = ((x * lax.rsqrt(ms + EPS)) * gain + shift).astype(BF16)

    def in_panels(first, count):
        for n in range(first, first + count):
            cs = slice(n * MXU_WIDTH, (n + 1) * MXU_WIDTH)
            z_ref[:, cs] = jnp.dot(h_ref[...], win_ref[:, cs], preferred_element_type=F32).astype(BF16)

    def out_panels(first, count):
        for n in range(first, first + count):
            cs = slice(n * MXU_WIDTH, (n + 1) * MXU_WIDTH)
            y_ref[:, cs] = jnp.dot(a_prev_ref[...], wout_ref[:, cs], preferred_element_type=F32)

    def finish(j):
        rows = slice(j * blk, (j + 1) * blk)
        xn = xp_ref[rows, :] + mod_ref[b_out, 2:3, :] * y_ref[rows, :]
        ms = jnp.mean(xn * xn, axis=-1, keepdims=True)
        o_ref[rows, :] = (xn * lax.rsqrt(ms + EPS)) * fg_ref[...]

    kv_cache = {}

    def block_kv(j):
        if j in kv_cache:
            return kv_cache[j]
        r0 = j * blk
        rows = slice(r0, r0 + blk)
        k_cur = z_ref[rows, off_k:off_k + d_kv]
        v_cur = z_ref[rows, off_v:off_v + d_kv]
        if j == 0:
            k_prev = kvp_ref[:, 0:d_kv]
            v_prev = kvp_ref[:, d_kv:2 * d_kv]
        else:
            k_prev = z_ref[r0 - blk:r0, off_k:off_k + d_kv]
            v_prev = z_ref[r0 - blk:r0, off_v:off_v + d_kv]
        kext = jnp.concatenate([k_prev, k_cur], axis=0).astype(F32) * scale
        kroll = pltpu.roll(kext, HEAD_DIM, 1)
        zero2 = jnp.zeros_like(kext)
        k_lo = [jnp.where(lo2, kext, zero2), jnp.where(lo2, kroll, zero2)]
        k_hi = [jnp.where(lo2, zero2, kroll), jnp.where(lo2, zero2, kext)]
        vext_t = jnp.concatenate([v_prev, v_cur], axis=0).astype(F32).T
        kv_cache[j] = (k_lo, k_hi, vext_t)
        return kv_cache[j]

    def attention(j, kv):
        rows = slice(j * blk, (j + 1) * blk)
        k_lo, k_hi, vext_t = block_kv(j)
        p0 = kv * pairs_per_kv
        kbd = jnp.concatenate([k_lo[kv], k_hi[kv]], axis=0).astype(BF16)
        q4 = jnp.concatenate(
            [z_ref[rows, (p0 + p) * LANES:(p0 + p + 1) * LANES] for p in range(pairs_per_kv)],
            axis=0)
        s_t = lax.dot_general(kbd, q4, (((1,), (1,)), ((), ())),
                              preferred_element_type=F32)
        yield
        v_t = vext_t[kv * HEAD_DIM:(kv + 1) * HEAD_DIM, :]
        v_t = jnp.where(keycol == 0, 0.0, v_t)
        v_aug = jnp.concatenate([v_t, jnp.ones((16, 2 * blk), F32)], axis=0).astype(BF16)

        pm = []
        for e in range(2):
            s_prev = s_t[2 * blk * e:2 * blk * e + blk, :]
            s_cur = s_t[2 * blk * e + blk:2 * blk * (e + 1), :]
            if j == 0:
                s_prev = s_prev + pen0
            sc = jnp.where(cur4, s_cur, s_prev)
            sink = jnp.concatenate(
                [jnp.full((1, blk), sinks_ref[2 * (p0 + p) + e], F32)
                 for p in range(pairs_per_kv)], axis=1)
            m = jnp.maximum(jnp.max(sc, axis=0, keepdims=True), sink)
            pr = jnp.exp(sc - m)
            p_sink = jnp.exp(sink - m)
            zero4 = jnp.zeros_like(pr)
            pm_prev = jnp.where(sink_row4, p_sink, jnp.where(cur4, zero4, pr))
            pm_cur = jnp.where(cur4, pr, zero4)
            pm.append(jnp.concatenate([pm_prev, pm_cur], axis=0).astype(BF16))

        att = []
        for p in range(pairs_per_kv):
            cs = slice(p * blk, (p + 1) * blk)
            pm_pair = jnp.concatenate([pm[0][:, cs], pm[1][:, cs]], axis=1)
            o_t = jnp.dot(v_aug, pm_pair, preferred_element_type=F32)
            inv = 1.0 / o_t[HEAD_DIM:HEAD_DIM + 8, :]
            inv = jnp.concatenate([inv] * (HEAD_DIM // 8), axis=0)
            att_t = o_t[0:HEAD_DIM, :] * inv
            att.append(jnp.concatenate([att_t[:, 0:blk], att_t[:, blk:2 * blk]], axis=0).T)
        yield
        for p in range(pairs_per_kv):
            c0 = (p0 + p) * LANES
            g = z_ref[rows, off_ga + c0:off_ga + c0 + LANES].astype(F32)
            a_cur_ref[rows, c0:c0 + LANES] = (att[p] * _silu(g)).astype(BF16)

    def spatial_gating(j):
        rows = slice(j * blk, (j + 1) * blk)
        vs = z_ref[rows, off_vs:off_vs + d_sgu].astype(F32)
        mu = jnp.mean(vs, axis=-1, keepdims=True)
        dv = vs - mu
        var = jnp.mean(dv * dv, axis=-1, keepdims=True)
        vn = ((dv * lax.rsqrt(var + EPS)) * lng_ref[...] + lnb_ref[...]).astype(BF16)
        yield
        mixed = []
        for g in range(SGU_GROUPS):
            c0 = g * LANES
            w = jnp.where(tri, sw_ref[g], 0.0).astype(BF16)
            mixed.append(jnp.dot(w, vn[:, c0:c0 + LANES], preferred_element_type=F32) + sbt_ref[:, g:g + 1])
        yield
        for g in range(SGU_GROUPS):
            c0 = g * LANES
            u = z_ref[rows, off_u + c0:off_u + c0 + LANES].astype(F32)
            gg = z_ref[rows, off_gs + c0:off_gs + c0 + LANES].astype(F32)
            a_cur_ref[rows, d_attn + c0:d_attn + c0 + LANES] = ((u * mixed[g]) * _silu(gg)).astype(BF16)

    assert n_blk == 2 and N_KV_HEADS == 2, "the emission schedule below is written for two blocks per tile"
    panel = {name: (off // MXU_WIDTH, width // MXU_WIDTH) for name, off, width in (
        ("q", 0, d_attn), ("kv", off_k, 2 * d_kv), ("ga", off_ga, d_attn),
        ("u", off_u, d_sgu), ("vs", off_vs, d_sgu), ("gs", off_gs, d_sgu))}
    assert all(c >= 1 for _, c in panel.values()) and d_in % MXU_WIDTH == 0

    n_out = d_model // MXU_WIDTH
    rest = [n for name in ("vs", "ga", "u", "gs") for n in range(panel[name][0], sum(panel[name]))]

    def fill(count):
        for _ in range(count):
            in_panels(rest.pop(0), 1)

    out_panels(0, n_out // 2)
    modulated_norm()
    in_panels(*panel["q"])
    in_panels(*panel["kv"])
    fill(1)
    att = [attention(j, kv) for j in range(n_blk) for kv in range(N_KV_HEADS)]
    sgu = [spatial_gating(j) for j in range(n_blk)]
    for i, piece in enumerate(att):
        next(piece)
        if i == 0:
            out_panels(n_out // 2, n_out - n_out // 2)
        else:
            fill(4)
        next(piece)
        if i == 0:
            for j in range(n_blk):
                finish(j)
    for piece in sgu:
        next(piece)
    fill(2)
    for piece in att:
        for _ in piece:
            pass
    for piece in sgu:
        next(piece)
    fill(len(rest))
    for piece in sgu:
        for _ in piece:
            pass

    a_prev_ref[...] = a_cur_ref[...]
    xp_ref[...] = x_ref[...]


def _layer(sinks, x2, mod3, norm_g, w_in_bf, ln_g, ln_b, sgu_w, sgu_bt, w_out_bf, final_g, seq, d_attn, d_sgu):
    m, d = x2.shape
    t = TILE_TOKENS
    d_in = w_in_bf.shape[1]
    d_mix = w_out_bf.shape[0]
    d_kv = N_KV_HEADS * HEAD_DIM
    n_tiles = m // t
    resident = pl.Buffered(1)

    def in_tile(s):
        return jnp.minimum(s, n_tiles - 1)

    def out_tile(s):
        return jnp.maximum(s - 1, 0)

    const2 = lambda s: (0, 0)
    const3 = lambda s: (0, 0, 0)
    return pl.pallas_call(
        functools.partial(_layer_kernel, d_attn=d_attn, d_sgu=d_sgu, steps_per_seq=seq // t, n_tiles=n_tiles),
        grid=(n_tiles + 1,),
        in_specs=[
            pl.BlockSpec(memory_space=pltpu.SMEM),
            pl.BlockSpec((t, d), lambda s: (in_tile(s), 0)),
            pl.BlockSpec(mod3.shape, const3),
            pl.BlockSpec((1, d), const2),
            pl.BlockSpec((d, d_in), const2, pipeline_mode=resident),
            pl.BlockSpec((1, d_sgu), const2),
            pl.BlockSpec((1, d_sgu), const2),
            pl.BlockSpec((SGU_GROUPS, WINDOW, WINDOW), const3),
            pl.BlockSpec((WINDOW, SGU_GROUPS), const2),
            pl.BlockSpec((d_mix, d), const2, pipeline_mode=resident),
            pl.BlockSpec((1, d), const2),
        ],
        out_specs=pl.BlockSpec((t, d), lambda s: (out_tile(s), 0)),
        out_shape=jax.ShapeDtypeStruct((m, d), F32),
        scratch_shapes=[
            pltpu.VMEM((t, d), BF16),
            pltpu.VMEM((t, d_in), BF16),
            pltpu.VMEM((WINDOW, 2 * d_kv), BF16),
            pltpu.VMEM((t, d_mix), BF16),
            pltpu.VMEM((t, d_mix), BF16),
            pltpu.VMEM((t, d), F32),
            pltpu.VMEM((t, d), F32),
        ],
        compiler_params=pltpu.CompilerParams(
            dimension_semantics=("arbitrary",), vmem_limit_bytes=VMEM_LIMIT_BYTES),
        name="layer",
    )(sinks, x2, mod3, norm_g, w_in_bf, ln_g, ln_b, sgu_w, sgu_bt, w_out_bf, final_g)


def kernel(x, c, norm_g, w_ada, b_ada, w_in, attn_sinks, sgu_ln_g, sgu_ln_b, sgu_w, sgu_b, w_out, final_g):
    batch, seq, d = x.shape
    depth = norm_g.shape[0]
    n_q_heads = attn_sinks.shape[1]
    d_attn = n_q_heads * HEAD_DIM
    d_sgu = sgu_ln_g.shape[1]
    assert sgu_w.shape[1:] == (SGU_GROUPS, WINDOW, WINDOW) and d_sgu == SGU_GROUPS * LANES
    assert depth == 1, "stacked layers need the un-normalised residual stream between layers"

    c_pad = jnp.pad(c, ((0, 8 - batch % 8), (0, 0))) if batch % 8 else c
    x2 = x.reshape(batch * seq, d)
    mod = _adaln(c_pad, w_ada[0], b_ada[0].reshape(1, -1), tn=512)[:batch]
    out = _layer(attn_sinks[0], x2, mod.reshape(batch, 3, d), norm_g[0].reshape(1, d), w_in[0].astype(BF16),
                 sgu_ln_g[0].reshape(1, d_sgu), sgu_ln_b[0].reshape(1, d_sgu), sgu_w[0], sgu_b[0].T,
                 w_out[0].astype(BF16), final_g.reshape(1, d), seq, d_attn, d_sgu)
    return out.reshape(batch, seq, d)
```

```python
import functools
import math

import jax
import jax.numpy as jnp
from jax import lax
from jax.experimental import pallas as pl
from jax.experimental.pallas import tpu as pltpu

HEAD_DIM = 64
N_KV_HEADS = 2
WINDOW = 128
SGU_GROUPS = 8
EPS = 1e-6

F32 = jnp.float32
BF16 = jnp.bfloat16

LANES = 128
MXU_WIDTH = 256
VMEM_LIMIT_BYTES = 60 * 1024 * 1024
TILE_TOKENS = 256


def _silu(v):
    return v / (1.0 + jnp.exp(-v))


def _adaln_kernel(c_ref, w_ref, b_ref, o_ref):
    c_act = _silu(c_ref[...])
    acc = jnp.dot(c_act.astype(BF16), w_ref[...].astype(BF16), preferred_element_type=F32)
    o_ref[...] = acc + b_ref[...]


def _adaln(c_pad, w_ada, b_ada, tn):
    rows, d = c_pad.shape
    n = w_ada.shape[1]
    return pl.pallas_call(
        _adaln_kernel,
        grid=(n // tn,),
        in_specs=[
            pl.BlockSpec((rows, d), lambda j: (0, 0)),
            pl.BlockSpec((d, tn), lambda j: (0, j)),
            pl.BlockSpec((1, tn), lambda j: (0, j)),
        ],
        out_specs=pl.BlockSpec((rows, tn), lambda j: (0, j)),
        out_shape=jax.ShapeDtypeStruct((rows, n), F32),
        compiler_params=pltpu.CompilerParams(
            dimension_semantics=("arbitrary",), vmem_limit_bytes=VMEM_LIMIT_BYTES),
        name="adaln",
    )(c_pad, w_ada, b_ada)


def _layer_kernel(sinks_ref, x_ref, xp_ref, mod_ref, ng_ref, win_ref, lng_ref, lnb_ref, sw_ref, sbt_ref,
                  wout_ref, fg_ref, o_ref,
                  h_ref, z_ref, kvp_ref, a_ref, y_ref,
                  *, d_attn, d_sgu, steps_per_seq, n_tiles):
    t, d_model = x_ref.shape
    s_id = pl.program_id(0)
    blk = WINDOW
    n_blk = t // blk
    d_kv = N_KV_HEADS * HEAD_DIM
    pairs_per_kv = d_attn // LANES // N_KV_HEADS
    rows4 = pairs_per_kv * blk

    off_k = d_attn
    off_v = off_k + d_kv
    off_ga = off_v + d_kv
    off_u = off_ga + d_attn
    off_vs = off_u + d_sgu
    off_gs = off_vs + d_sgu
    d_in = off_gs + d_sgu

    b_in = jnp.minimum(s_id, n_tiles - 1) // steps_per_seq
    b_out = jnp.maximum(s_id - 1, 0) // steps_per_seq
    first_in_seq = lax.rem(s_id, steps_per_seq) == 0
    pen0 = jnp.where(first_in_seq, -jnp.inf, 0.0).astype(F32)

    @pl.when(s_id == 0)
    def _():
        a_ref[...] = jnp.zeros_like(a_ref)
        z_ref[t - blk:t, off_k:off_k + 2 * d_kv] = jnp.zeros((blk, 2 * d_kv), BF16)

    lane2 = lax.broadcasted_iota(jnp.int32, (2 * blk, LANES), 1)
    lo2 = lane2 < HEAD_DIM
    row = lax.broadcasted_iota(jnp.int32, (blk, blk), 0)
    col = lax.broadcasted_iota(jnp.int32, (blk, blk), 1)
    tri = col <= row
    key4 = lax.broadcasted_iota(jnp.int32, (blk, rows4), 0)
    qry4 = lax.broadcasted_iota(jnp.int32, (blk, rows4), 1) & (blk - 1)
    cur4 = key4 <= qry4
    sink_row4 = key4 == 0
    keycol = lax.broadcasted_iota(jnp.int32, (HEAD_DIM, 2 * blk), 1)
    scale = 1.0 / math.sqrt(HEAD_DIM)

    def modulated_norm():
        shift = mod_ref[b_in, 0:1, :]
        gain = ng_ref[...] * (1.0 + mod_ref[b_in, 1:2, :])
        for r in range(0, t, blk):
            x = x_ref[r:r + blk, :]
            ms = jnp.mean(x * x, axis=-1, keepdims=True)
            h_ref[r:r + blk, :] = ((x * lax.rsqrt(ms + EPS)) * gain + shift).astype(BF16)

    def in_panels(first, count):
        for n in range(first, first + count):
            cs = slice(n * MXU_WIDTH, (n + 1) * MXU_WIDTH)
            z_ref[:, cs] = jnp.dot(h_ref[...], win_ref[:, cs], preferred_element_type=F32).astype(BF16)

    def out_panels(first, count):
        for n in range(first, first + count):
            cs = slice(n * MXU_WIDTH, (n + 1) * MXU_WIDTH)
            y_ref[:, cs] = jnp.dot(a_ref[...], wout_ref[:, cs], preferred_element_type=F32)

    def finish(j):
        rows = slice(j * blk, (j + 1) * blk)
        xn = xp_ref[rows, :] + mod_ref[b_out, 2:3, :] * y_ref[rows, :]
        ms = jnp.mean(xn * xn, axis=-1, keepdims=True)
        o_ref[rows, :] = (xn * lax.rsqrt(ms + EPS)) * fg_ref[...]

    kv_cache = {}

    def block_kv(j):
        if j in kv_cache:
            return kv_cache[j]
        r0 = j * blk
        rows = slice(r0, r0 + blk)
        k_cur = z_ref[rows, off_k:off_k + d_kv]
        v_cur = z_ref[rows, off_v:off_v + d_kv]
        if j == 0:
            k_prev = kvp_ref[:, 0:d_kv]
            v_prev = kvp_ref[:, d_kv:2 * d_kv]
        else:
            k_prev = z_ref[r0 - blk:r0, off_k:off_k + d_kv]
            v_prev = z_ref[r0 - blk:r0, off_v:off_v + d_kv]
        kext = jnp.concatenate([k_prev, k_cur], axis=0).astype(F32) * scale
        kroll = pltpu.roll(kext, HEAD_DIM, 1)
        zero2 = jnp.zeros_like(kext)
        k_lo = [jnp.where(lo2, kext, zero2), jnp.where(lo2, kroll, zero2)]
        k_hi = [jnp.where(lo2, zero2, kroll), jnp.where(lo2, zero2, kext)]
        vext_t = jnp.concatenate([v_prev, v_cur], axis=0).astype(F32).T
        kv_cache[j] = (k_lo, k_hi, vext_t)
        return kv_cache[j]

    def attention(j, kv):
        rows = slice(j * blk, (j + 1) * blk)
        k_lo, k_hi, vext_t = block_kv(j)
        p0 = kv * pairs_per_kv
        kbd = jnp.concatenate([k_lo[kv], k_hi[kv]], axis=0).astype(BF16)
        q4 = jnp.concatenate(
            [z_ref[rows, (p0 + p) * LANES:(p0 + p + 1) * LANES] for p in range(pairs_per_kv)],
            axis=0)
        s_t = lax.dot_general(kbd, q4, (((1,), (1,)), ((), ())),
                              preferred_element_type=F32)
        yield
        v_t = vext_t[kv * HEAD_DIM:(kv + 1) * HEAD_DIM, :]
        v_t = jnp.where(keycol == 0, 0.0, v_t)
        v_aug = jnp.concatenate([v_t, jnp.ones((16, 2 * blk), F32)], axis=0).astype(BF16)

        pm = []
        for e in range(2):
            s_prev = s_t[2 * blk * e:2 * blk * e + blk, :]
            s_cur = s_t[2 * blk * e + blk:2 * blk * (e + 1), :]
            if j == 0:
                s_prev = s_prev + pen0
            sc = jnp.where(cur4, s_cur, s_prev)
            sink = jnp.concatenate(
                [jnp.full((1, blk), sinks_ref[2 * (p0 + p) + e], F32)
                 for p in range(pairs_per_kv)], axis=1)
            m = jnp.maximum(jnp.max(sc, axis=0, keepdims=True), sink)
            pr = jnp.exp(sc - m)
            p_sink = jnp.exp(sink - m)
            zero4 = jnp.zeros_like(pr)
            pm_prev = jnp.where(sink_row4, p_sink, jnp.where(cur4, zero4, pr))
            pm_cur = jnp.where(cur4, pr, zero4)
            pm.append(jnp.concatenate([pm_prev, pm_cur], axis=0).astype(BF16))

        att = []
        for p in range(pairs_per_kv):
            cs = slice(p * blk, (p + 1) * blk)
            pm_pair = jnp.concatenate([pm[0][:, cs], pm[1][:, cs]], axis=1)
            o_t = jnp.dot(v_aug, pm_pair, preferred_element_type=F32)
            inv = 1.0 / o_t[HEAD_DIM:HEAD_DIM + 8, :]
            inv = jnp.concatenate([inv] * (HEAD_DIM // 8), axis=0)
            att_t = o_t[0:HEAD_DIM, :] * inv
            att.append(jnp.concatenate([att_t[:, 0:blk], att_t[:, blk:2 * blk]], axis=0).T)
        yield
        for p in range(pairs_per_kv):
            c0 = (p0 + p) * LANES
            g = z_ref[rows, off_ga + c0:off_ga + c0 + LANES].astype(F32)
            a_ref[rows, c0:c0 + LANES] = (att[p] * _silu(g)).astype(BF16)

    def spatial_gating(j):
        rows = slice(j * blk, (j + 1) * blk)
        vs = z_ref[rows, off_vs:off_vs + d_sgu].astype(F32)
        mu = jnp.mean(vs, axis=-1, keepdims=True)
        dv = vs - mu
        var = jnp.mean(dv * dv, axis=-1, keepdims=True)
        vn = ((dv * lax.rsqrt(var + EPS)) * lng_ref[...] + lnb_ref[...]).astype(BF16)
        yield
        mixed = []
        for g in range(SGU_GROUPS):
            c0 = g * LANES
            w = jnp.where(tri, sw_ref[g], 0.0).astype(BF16)
            mixed.append(jnp.dot(w, vn[:, c0:c0 + LANES], preferred_element_type=F32) + sbt_ref[:, g:g + 1])
        yield
        for g in range(SGU_GROUPS):
            c0 = g * LANES
            u = z_ref[rows, off_u + c0:off_u + c0 + LANES].astype(F32)
            gg = z_ref[rows, off_gs + c0:off_gs + c0 + LANES].astype(F32)
            a_ref[rows, d_attn + c0:d_attn + c0 + LANES] = ((u * mixed[g]) * _silu(gg)).astype(BF16)

    assert n_blk == 2 and N_KV_HEADS == 2, "the emission schedule below is written for two blocks per tile"
    panel = {name: (off // MXU_WIDTH, width // MXU_WIDTH) for name, off, width in (
        ("q", 0, d_attn), ("kv", off_k, 2 * d_kv), ("ga", off_ga, d_attn),
        ("u", off_u, d_sgu), ("vs", off_vs, d_sgu), ("gs", off_gs, d_sgu))}
    assert all(c >= 1 for _, c in panel.values()) and d_in % MXU_WIDTH == 0

    n_out = d_model // MXU_WIDTH

    @pl.when(s_id < n_tiles)
    def _():
        kvp_ref[...] = z_ref[t - blk:t, off_k:off_k + 2 * d_kv]
        rest = [n for name in ("vs", "ga", "u", "gs") for n in range(panel[name][0], sum(panel[name]))]

        def fill(count):
            for _ in range(count):
                in_panels(rest.pop(0), 1)

        out_panels(0, n_out // 2)
        modulated_norm()
        in_panels(*panel["q"])
        in_panels(*panel["kv"])
        fill(1)
        att = [attention(j, kv) for j in range(n_blk) for kv in range(N_KV_HEADS)]
        sgu = [spatial_gating(j) for j in range(n_blk)]
        for i, piece in enumerate(att):
            next(piece)
            if i == 0:
                out_panels(n_out // 2, n_out - n_out // 2)
            else:
                fill(4)
            next(piece)
            if i == 0:
                for j in range(n_blk):
                    finish(j)
        for piece in sgu:
            next(piece)
        fill(2)
        for piece in att:
            for _ in piece:
                pass
        for piece in sgu:
            next(piece)
        fill(len(rest))
        for piece in sgu:
            for _ in piece:
                pass

    @pl.when(s_id == n_tiles)
    def _():
        out_panels(0, n_out)
        for j in range(n_blk):
            finish(j)


def _pad_lanes(w):
    return jnp.pad(w, ((0, 0), (0, LANES)))


def _layer(sinks, x2, mod3, norm_g, w_in_bf, ln_g, ln_b, sgu_w, sgu_bt, w_out_bf, final_g, seq, d_attn, d_sgu):
    m, d = x2.shape
    t = TILE_TOKENS
    d_in = w_in_bf.shape[1]
    d_mix = w_out_bf.shape[0]
    d_kv = N_KV_HEADS * HEAD_DIM
    n_tiles = m // t
    resident = pl.Buffered(1)

    def in_tile(s):
        return jnp.minimum(s, n_tiles - 1)

    def out_tile(s):
        return jnp.maximum(s - 1, 0)

    const2 = lambda s: (0, 0)
    const3 = lambda s: (0, 0, 0)
    return pl.pallas_call(
        functools.partial(_layer_kernel, d_attn=d_attn, d_sgu=d_sgu, steps_per_seq=seq // t, n_tiles=n_tiles),
        grid=(n_tiles + 1,),
        in_specs=[
            pl.BlockSpec(memory_space=pltpu.SMEM),
            pl.BlockSpec((t, d), lambda s: (in_tile(s), 0)),
            pl.BlockSpec((t, d), lambda s: (out_tile(s), 0)),
            pl.BlockSpec(mod3.shape, const3),
            pl.BlockSpec((1, d), const2),
            pl.BlockSpec((d, d_in), const2, pipeline_mode=resident),
            pl.BlockSpec((1, d_sgu), const2),
            pl.BlockSpec((1, d_sgu), const2),
            pl.BlockSpec((SGU_GROUPS, WINDOW, WINDOW), const3),
            pl.BlockSpec((WINDOW, SGU_GROUPS), const2),
            pl.BlockSpec(w_out_bf.shape, const2, pipeline_mode=resident),
            pl.BlockSpec((1, d), const2),
        ],
        out_specs=pl.BlockSpec((t, d), lambda s: (out_tile(s), 0)),
        out_shape=jax.ShapeDtypeStruct((m, d), F32),
        scratch_shapes=[
            pltpu.VMEM((t, d), BF16),
            pltpu.VMEM((t, d_in), BF16),
            pltpu.VMEM((WINDOW, 2 * d_kv), BF16),
            pltpu.VMEM((t, d_mix), BF16),
            pltpu.VMEM((t, d), F32),
        ],
        compiler_params=pltpu.CompilerParams(
            dimension_semantics=("arbitrary",), vmem_limit_bytes=VMEM_LIMIT_BYTES),
        name="layer",
    )(sinks, x2, x2, mod3, norm_g, w_in_bf, ln_g, ln_b, sgu_w, sgu_bt, w_out_bf, final_g)


def kernel(x, c, norm_g, w_ada, b_ada, w_in, attn_sinks, sgu_ln_g, sgu_ln_b, sgu_w, sgu_b, w_out, final_g):
    batch, seq, d = x.shape
    depth = norm_g.shape[0]
    n_q_heads = attn_sinks.shape[1]
    d_attn = n_q_heads * HEAD_DIM
    d_sgu = sgu_ln_g.shape[1]
    assert sgu_w.shape[1:] == (SGU_GROUPS, WINDOW, WINDOW) and d_sgu == SGU_GROUPS * LANES
    assert depth == 1, "stacked layers need the un-normalised residual stream between layers"

    c_pad = jnp.pad(c, ((0, 8 - batch % 8), (0, 0))) if batch % 8 else c
    x2 = x.reshape(batch * seq, d)
    mod = _adaln(c_pad, w_ada[0], b_ada[0].reshape(1, -1), tn=512)[:batch]
    out = _layer(attn_sinks[0], x2, mod.reshape(batch, 3, d), norm_g[0].reshape(1, d), w_in[0].astype(BF16),
                 sgu_ln_g[0].reshape(1, d_sgu), sgu_ln_b[0].reshape(1, d_sgu), sgu_w[0], sgu_b[0].T,
                 _pad_lanes(w_out[0].astype(BF16)), final_g.reshape(1, d), seq, d_attn, d_sgu)
    return out.reshape(batch, seq, d)
```

```python
import functools
import math

import jax
import jax.numpy as jnp
from jax import lax
from jax.experimental import pallas as pl
from jax.experimental.pallas import tpu as pltpu

HEAD_DIM = 64
N_KV_HEADS = 2
WINDOW = 128
SGU_GROUPS = 8
EPS = 1e-6

F32 = jnp.float32
BF16 = jnp.bfloat16

LANES = 128
MXU_WIDTH = 256
VMEM_LIMIT_BYTES = 60 * 1024 * 1024
TILE_TOKENS = 256
STAGE_ROWS = 128


def _silu(v):
    return v / (1.0 + jnp.exp(-v))


def _adaln_kernel(c_ref, w_ref, b_ref, o_ref):
    c_act = _silu(c_ref[...])
    acc = jnp.dot(c_act.astype(BF16), w_ref[...].astype(BF16), preferred_element_type=F32)
    o_ref[...] = acc + b_ref[...]


def _adaln(c_pad, w_ada, b_ada, tn):
    rows, d = c_pad.shape
    n = w_ada.shape[1]
    return pl.pallas_call(
        _adaln_kernel,
        grid=(n // tn,),
        in_specs=[
            pl.BlockSpec((rows, d), lambda j: (0, 0)),
            pl.BlockSpec((d, tn), lambda j: (0, j)),
            pl.BlockSpec((1, tn), lambda j: (0, j)),
        ],
        out_specs=pl.BlockSpec((rows, tn), lambda j: (0, j)),
        out_shape=jax.ShapeDtypeStruct((rows, n), F32),
        compiler_params=pltpu.CompilerParams(
            dimension_semantics=("arbitrary",), vmem_limit_bytes=VMEM_LIMIT_BYTES),
        name="adaln",
    )(c_pad, w_ada, b_ada)


def _stage_weight(w_hbm, w_ref, stage_ref, sem_ref):
    k, n = w_hbm.shape
    chunk = stage_ref.shape[1]
    n_chunks = k // chunk

    def copy(i):
        slot = i % 2
        return pltpu.make_async_copy(w_hbm.at[pl.ds(i * chunk, chunk), :],
                                     stage_ref.at[slot, :, pl.ds(0, n)], sem_ref.at[slot])

    copy(0).start()
    for i in range(n_chunks):
        if i + 1 < n_chunks:
            copy(i + 1).start()
        copy(i).wait()
        w_ref[i * chunk:(i + 1) * chunk, 0:n] = stage_ref[i % 2, :, 0:n].astype(BF16)


def _layer_kernel(sinks_ref, x_ref, xp_ref, mod_ref, ng_ref, win_hbm, lng_ref, lnb_ref, sw_ref, sbt_ref,
                  wout_hbm, fg_ref, o_ref,
                  win_ref, wout_ref, stage_ref, sem_ref, h_ref, z_ref, kvp_ref, a_ref, y_ref,
                  *, d_attn, d_sgu, steps_per_seq, n_tiles):
    t, d_model = x_ref.shape
    s_id = pl.program_id(0)
    blk = WINDOW
    n_blk = t // blk
    d_kv = N_KV_HEADS * HEAD_DIM
    pairs_per_kv = d_attn // LANES // N_KV_HEADS
    rows4 = pairs_per_kv * blk

    off_k = d_attn
    off_v = off_k + d_kv
    off_ga = off_v + d_kv
    off_u = off_ga + d_attn
    off_vs = off_u + d_sgu
    off_gs = off_vs + d_sgu
    d_in = off_gs + d_sgu

    b_in = jnp.minimum(s_id, n_tiles - 1) // steps_per_seq
    b_out = jnp.maximum(s_id - 1, 0) // steps_per_seq
    first_in_seq = lax.rem(s_id, steps_per_seq) == 0
    pen0 = jnp.where(first_in_seq, -jnp.inf, 0.0).astype(F32)

    @pl.when(s_id == 0)
    def _():
        a_ref[...] = jnp.zeros_like(a_ref)
        z_ref[t - blk:t, off_k:off_k + 2 * d_kv] = jnp.zeros((blk, 2 * d_kv), BF16)
        _stage_weight(wout_hbm, wout_ref, stage_ref, sem_ref)
        _stage_weight(win_hbm, win_ref, stage_ref, sem_ref)

    lane2 = lax.broadcasted_iota(jnp.int32, (2 * blk, LANES), 1)
    lo2 = lane2 < HEAD_DIM
    row = lax.broadcasted_iota(jnp.int32, (blk, blk), 0)
    col = lax.broadcasted_iota(jnp.int32, (blk, blk), 1)
    tri = col <= row
    key4 = lax.broadcasted_iota(jnp.int32, (blk, rows4), 0)
    qry4 = lax.broadcasted_iota(jnp.int32, (blk, rows4), 1) & (blk - 1)
    cur4 = key4 <= qry4
    sink_row4 = key4 == 0
    keycol = lax.broadcasted_iota(jnp.int32, (HEAD_DIM, 2 * blk), 1)
    scale = 1.0 / math.sqrt(HEAD_DIM)

    def modulated_norm():
        shift = mod_ref[b_in, 0:1, :]
        gain = ng_ref[...] * (1.0 + mod_ref[b_in, 1:2, :])
        for r in range(0, t, blk):
            x = x_ref[r:r + blk, :]
            ms = jnp.mean(x * x, axis=-1, keepdims=True)
            h_ref[r:r + blk, :] = ((x * lax.rsqrt(ms + EPS)) * gain + shift).astype(BF16)

    def in_panels(first, count):
        for n in range(first, first + count):
            cs = slice(n * MXU_WIDTH, (n + 1) * MXU_WIDTH)
            z_ref[:, cs] = jnp.dot(h_ref[...], win_ref[:, cs], preferred_element_type=F32).astype(BF16)

    def out_panels(first, count):
        for n in range(first, first + count):
            cs = slice(n * MXU_WIDTH, (n + 1) * MXU_WIDTH)
            y_ref[:, cs] = jnp.dot(a_ref[...], wout_ref[:, cs], preferred_element_type=F32)

    def finish(j):
        rows = slice(j * blk, (j + 1) * blk)
        xn = xp_ref[rows, :] + mod_ref[b_out, 2:3, :] * y_ref[rows, :]
        ms = jnp.mean(xn * xn, axis=-1, keepdims=True)
        o_ref[rows, :] = (xn * lax.rsqrt(ms + EPS)) * fg_ref[...]

    kv_cache = {}

    def block_kv(j):
        if j in kv_cache:
            return kv_cache[j]
        r0 = j * blk
        rows = slice(r0, r0 + blk)
        k_cur = z_ref[rows, off_k:off_k + d_kv]
        v_cur = z_ref[rows, off_v:off_v + d_kv]
        if j == 0:
            k_prev = kvp_ref[:, 0:d_kv]
            v_prev = kvp_ref[:, d_kv:2 * d_kv]
        else:
            k_prev = z_ref[r0 - blk:r0, off_k:off_k + d_kv]
            v_prev = z_ref[r0 - blk:r0, off_v:off_v + d_kv]
        kext = jnp.concatenate([k_prev, k_cur], axis=0).astype(F32) * scale
        kroll = pltpu.roll(kext, HEAD_DIM, 1)
        zero2 = jnp.zeros_like(kext)
        k_lo = [jnp.where(lo2, kext, zero2), jnp.where(lo2, kroll, zero2)]
        k_hi = [jnp.where(lo2, zero2, kroll), jnp.where(lo2, zero2, kext)]
        vext_t = jnp.concatenate([v_prev, v_cur], axis=0).astype(F32).T
        kv_cache[j] = (k_lo, k_hi, vext_t)
        return kv_cache[j]

    def attention(j, kv):
        rows = slice(j * blk, (j + 1) * blk)
        k_lo, k_hi, vext_t = block_kv(j)
        p0 = kv * pairs_per_kv
        kbd = jnp.concatenate([k_lo[kv], k_hi[kv]], axis=0).astype(BF16)
        q4 = jnp.concatenate(
            [z_ref[rows, (p0 + p) * LANES:(p0 + p + 1) * LANES] for p in range(pairs_per_kv)],
            axis=0)
        s_t = lax.dot_general(kbd, q4, (((1,), (1,)), ((), ())),
                              preferred_element_type=F32)
        yield
        v_t = vext_t[kv * HEAD_DIM:(kv + 1) * HEAD_DIM, :]
        v_t = jnp.where(keycol == 0, 0.0, v_t)
        v_aug = jnp.concatenate([v_t, jnp.ones((16, 2 * blk), F32)], axis=0).astype(BF16)

        pm = []
        for e in range(2):
            s_prev = s_t[2 * blk * e:2 * blk * e + blk, :]
            s_cur = s_t[2 * blk * e + blk:2 * blk * (e + 1), :]
            if j == 0:
                s_prev = s_prev + pen0
            sc = jnp.where(cur4, s_cur, s_prev)
            sink = jnp.concatenate(
                [jnp.full((1, blk), sinks_ref[2 * (p0 + p) + e], F32)
                 for p in range(pairs_per_kv)], axis=1)
            m = jnp.maximum(jnp.max(sc, axis=0, keepdims=True), sink)
            pr = jnp.exp(sc - m)
            p_sink = jnp.exp(sink - m)
            zero4 = jnp.zeros_like(pr)
            pm_prev = jnp.where(sink_row4, p_sink, jnp.where(cur4, zero4, pr))
            pm_cur = jnp.where(cur4, pr, zero4)
            pm.append(jnp.concatenate([pm_prev, pm_cur], axis=0).astype(BF16))

        att = []
        for p in range(pairs_per_kv):
            cs = slice(p * blk, (p + 1) * blk)
            pm_pair = jnp.concatenate([pm[0][:, cs], pm[1][:, cs]], axis=1)
            o_t = jnp.dot(v_aug, pm_pair, preferred_element_type=F32)
            inv = 1.0 / o_t[HEAD_DIM:HEAD_DIM + 8, :]
            inv = jnp.concatenate([inv] * (HEAD_DIM // 8), axis=0)
            att_t = o_t[0:HEAD_DIM, :] * inv
            att.append(jnp.concatenate([att_t[:, 0:blk], att_t[:, blk:2 * blk]], axis=0).T)
        yield
        for p in range(pairs_per_kv):
            c0 = (p0 + p) * LANES
            g = z_ref[rows, off_ga + c0:off_ga + c0 + LANES].astype(F32)
            a_ref[rows, c0:c0 + LANES] = (att[p] * _silu(g)).astype(BF16)

    def spatial_gating(j):
        rows = slice(j * blk, (j + 1) * blk)
        vs = z_ref[rows, off_vs:off_vs + d_sgu].astype(F32)
        mu = jnp.mean(vs, axis=-1, keepdims=True)
        dv = vs - mu
        var = jnp.mean(dv * dv, axis=-1, keepdims=True)
        vn = ((dv * lax.rsqrt(var + EPS)) * lng_ref[...] + lnb_ref[...]).astype(BF16)
        yield
        mixed = []
        for g in range(SGU_GROUPS):
            c0 = g * LANES
            w = jnp.where(tri, sw_ref[g], 0.0).astype(BF16)
            mixed.append(jnp.dot(w, vn[:, c0:c0 + LANES], preferred_element_type=F32) + sbt_ref[:, g:g + 1])
        yield
        for g in range(SGU_GROUPS):
            c0 = g * LANES
            u = z_ref[rows, off_u + c0:off_u + c0 + LANES].astype(F32)
            gg = z_ref[rows, off_gs + c0:off_gs + c0 + LANES].astype(F32)
            a_ref[rows, d_attn + c0:d_attn + c0 + LANES] = ((u * mixed[g]) * _silu(gg)).astype(BF16)

    assert n_blk == 2 and N_KV_HEADS == 2, "the emission schedule below is written for two blocks per tile"
    panel = {name: (off // MXU_WIDTH, width // MXU_WIDTH) for name, off, width in (
        ("q", 0, d_attn), ("kv", off_k, 2 * d_kv), ("ga", off_ga, d_attn),
        ("u", off_u, d_sgu), ("vs", off_vs, d_sgu), ("gs", off_gs, d_sgu))}
    assert all(c >= 1 for _, c in panel.values()) and d_in % MXU_WIDTH == 0

    n_out = d_model // MXU_WIDTH

    @pl.when(s_id < n_tiles)
    def _():
        kvp_ref[...] = z_ref[t - blk:t, off_k:off_k + 2 * d_kv]
        rest = [n for name in ("vs", "ga", "u", "gs") for n in range(panel[name][0], sum(panel[name]))]

        def fill(count):
            for _ in range(count):
                in_panels(rest.pop(0), 1)

        out_panels(0, n_out // 2)
        modulated_norm()
        in_panels(*panel["q"])
        in_panels(*panel["kv"])
        fill(1)
        att = [attention(j, kv) for j in range(n_blk) for kv in range(N_KV_HEADS)]
        sgu = [spatial_gating(j) for j in range(n_blk)]
        for i, piece in enumerate(att):
            next(piece)
            if i == 0:
                out_panels(n_out // 2, n_out - n_out // 2)
            else:
                fill(4)
            next(piece)
            if i == 0:
                for j in range(n_blk):
                    finish(j)
        for piece in sgu:
            next(piece)
        fill(2)
        for piece in att:
            for _ in piece:
                pass
        for piece in sgu:
            next(piece)
        fill(len(rest))
        for piece in sgu:
            for _ in piece:
                pass

    @pl.when(s_id == n_tiles)
    def _():
        out_panels(0, n_out)
        for j in range(n_blk):
            finish(j)


def _layer(sinks, x2, mod3, norm_g, w_in, ln_g, ln_b, sgu_w, sgu_bt, w_out, final_g, seq, d_attn, d_sgu):
    m, d = x2.shape
    t = TILE_TOKENS
    d_in = w_in.shape[1]
    d_mix = w_out.shape[0]
    d_kv = N_KV_HEADS * HEAD_DIM
    n_tiles = m // t
    assert d % STAGE_ROWS == 0 and d_mix % STAGE_ROWS == 0 and d <= d_in

    def in_tile(s):
        return jnp.minimum(s, n_tiles - 1)

    def out_tile(s):
        return jnp.maximum(s - 1, 0)

    const2 = lambda s: (0, 0)
    const3 = lambda s: (0, 0, 0)
    return pl.pallas_call(
        functools.partial(_layer_kernel, d_attn=d_attn, d_sgu=d_sgu, steps_per_seq=seq // t, n_tiles=n_tiles),
        grid=(n_tiles + 1,),
        in_specs=[
            pl.BlockSpec(memory_space=pltpu.SMEM),
            pl.BlockSpec((t, d), lambda s: (in_tile(s), 0)),
            pl.BlockSpec((t, d), lambda s: (out_tile(s), 0)),
            pl.BlockSpec(mod3.shape, const3),
            pl.BlockSpec((1, d), const2),
            pl.BlockSpec(memory_space=pl.ANY),
            pl.BlockSpec((1, d_sgu), const2),
            pl.BlockSpec((1, d_sgu), const2),
            pl.BlockSpec((SGU_GROUPS, WINDOW, WINDOW), const3),
            pl.BlockSpec((WINDOW, SGU_GROUPS), const2),
            pl.BlockSpec(memory_space=pl.ANY),
            pl.BlockSpec((1, d), const2),
        ],
        out_specs=pl.BlockSpec((t, d), lambda s: (out_tile(s), 0)),
        out_shape=jax.ShapeDtypeStruct((m, d), F32),
        scratch_shapes=[
            pltpu.VMEM((d, d_in), BF16),
            pltpu.VMEM((d_mix, d + LANES), BF16),
            pltpu.VMEM((2, STAGE_ROWS, d_in), F32),
            pltpu.SemaphoreType.DMA((2,)),
            pltpu.VMEM((t, d), BF16),
            pltpu.VMEM((t, d_in), BF16),
            pltpu.VMEM((WINDOW, 2 * d_kv), BF16),
            pltpu.VMEM((t, d_mix), BF16),
            pltpu.VMEM((t, d), F32),
        ],
        compiler_params=pltpu.CompilerParams(
            dimension_semantics=("arbitrary",), vmem_limit_bytes=VMEM_LIMIT_BYTES),
        name="layer",
    )(sinks, x2, x2, mod3, norm_g, w_in, ln_g, ln_b, sgu_w, sgu_bt, w_out, final_g)


def kernel(x, c, norm_g, w_ada, b_ada, w_in, attn_sinks, sgu_ln_g, sgu_ln_b, sgu_w, sgu_b, w_out, final_g):
    batch, seq, d = x.shape
    depth = norm_g.shape[0]
    n_q_heads = attn_sinks.shape[1]
    d_attn = n_q_heads * HEAD_DIM
    d_sgu = sgu_ln_g.shape[1]
    assert sgu_w.shape[1:] == (SGU_GROUPS, WINDOW, WINDOW) and d_sgu == SGU_GROUPS * LANES
    assert depth == 1, "stacked layers need the un-normalised residual stream between layers"

    c_pad = jnp.pad(c, ((0, 8 - batch % 8), (0, 0))) if batch % 8 else c
    x2 = x.reshape(batch * seq, d)
    mod = _adaln(c_pad, w_ada[0], b_ada[0].reshape(1, -1), tn=512)[:batch]
    out = _layer(attn_sinks[0], x2, mod.reshape(batch, 3, d), norm_g[0].reshape(1, d), w_in[0],
                 sgu_ln_g[0].reshape(1, d_sgu), sgu_ln_b[0].reshape(1, d_sgu), sgu_w[0], sgu_b[0].T,
                 w_out[0], final_g.reshape(1, d), seq, d_attn, d_sgu)
    return out.reshape(batch, seq, d)
```

```python
import functools
import math

import jax
import jax.numpy as jnp
from jax import lax
from jax.experimental import pallas as pl
from jax.experimental.pallas import tpu as pltpu

HEAD_DIM = 64
N_KV_HEADS = 2
WINDOW = 128
SGU_GROUPS = 8
EPS = 1e-6

F32 = jnp.float32
BF16 = jnp.bfloat16

LANES = 128
MXU_WIDTH = 256
VMEM_LIMIT_BYTES = 60 * 1024 * 1024
TILE_TOKENS = 256
STAGE_ROWS = 128
STAGE_COLS = 3072
STAGE_SLOTS = 3


def _silu(v):
    return v / (1.0 + jnp.exp(-v))


def _stream_weight(w_hbm, stage_ref, sem_ref, consume):
    k, n = w_hbm.shape
    col_parts = pl.cdiv(n, STAGE_COLS)
    width = n // col_parts
    assert k % STAGE_ROWS == 0 and n % col_parts == 0 and width % LANES == 0
    pieces = [(i, p) for i in range(k // STAGE_ROWS) for p in range(col_parts)]

    def copy(j):
        i, p = pieces[j]
        slot = j % STAGE_SLOTS
        return pltpu.make_async_copy(
            w_hbm.at[pl.ds(i * STAGE_ROWS, STAGE_ROWS), pl.ds(p * width, width)],
            stage_ref.at[slot, :, pl.ds(0, width)], sem_ref.at[slot])

    ahead = STAGE_SLOTS - 1
    for j in range(min(ahead, len(pieces))):
        copy(j).start()
    for j, (i, p) in enumerate(pieces):
        copy(j).wait()
        consume(slice(i * STAGE_ROWS, (i + 1) * STAGE_ROWS), slice(p * width, (p + 1) * width),
                stage_ref[j % STAGE_SLOTS, :, 0:width])
        if j + ahead < len(pieces):
            copy(j + ahead).start()


def _layer_kernel(sinks_ref, x_ref, xp_ref, c_ref, wada_hbm, bada_ref, ng_ref, win_hbm, lng_ref, lnb_ref,
                  sw_ref, sbt_ref, wout_hbm, fg_ref, o_ref,
                  mod_ref, win_ref, wout_ref, stage_ref, sem_ref, h_ref, z_ref, kvp_ref, a_ref, y_ref,
                  *, d_attn, d_sgu, steps_per_seq, n_tiles):
    t, d_model = x_ref.shape
    s_id = pl.program_id(0)
    blk = WINDOW
    n_blk = t // blk
    d_kv = N_KV_HEADS * HEAD_DIM
    pairs_per_kv = d_attn // LANES // N_KV_HEADS
    rows4 = pairs_per_kv * blk

    off_k = d_attn
    off_v = off_k + d_kv
    off_ga = off_v + d_kv
    off_u = off_ga + d_attn
    off_vs = off_u + d_sgu
    off_gs = off_vs + d_sgu
    d_in = off_gs + d_sgu

    b_in = jnp.minimum(s_id, n_tiles - 1) // steps_per_seq
    b_out = jnp.maximum(s_id - 1, 0) // steps_per_seq
    first_in_seq = lax.rem(s_id, steps_per_seq) == 0
    pen0 = jnp.where(first_in_seq, -jnp.inf, 0.0).astype(F32)

    @pl.when(s_id == 0)
    def _():
        a_ref[...] = jnp.zeros_like(a_ref)
        z_ref[t - blk:t, off_k:off_k + 2 * d_kv] = jnp.zeros((blk, 2 * d_kv), BF16)

        batch = c_ref.shape[0]
        c_act = _silu(c_ref[...])
        c_act = jnp.concatenate([c_act, jnp.zeros((mod_ref.shape[0] - batch, d_model), F32)], axis=0).astype(BF16)
        mod_ref[...] = jnp.broadcast_to(bada_ref[...], mod_ref.shape)

        def ada_piece(rows, cols, piece):
            mod_ref[:, cols] += jnp.dot(c_act[:, rows], piece.astype(BF16), preferred_element_type=F32)

        def win_piece(rows, cols, piece):
            win_ref[rows, cols] = piece.astype(BF16)

        def wout_piece(rows, cols, piece):
            wout_ref[rows, cols] = piece.astype(BF16)

        _stream_weight(wada_hbm, stage_ref, sem_ref, ada_piece)
        _stream_weight(win_hbm, stage_ref, sem_ref, win_piece)
        _stream_weight(wout_hbm, stage_ref, sem_ref, wout_piece)

    lane2 = lax.broadcasted_iota(jnp.int32, (2 * blk, LANES), 1)
    lo2 = lane2 < HEAD_DIM
    row = lax.broadcasted_iota(jnp.int32, (blk, blk), 0)
    col = lax.broadcasted_iota(jnp.int32, (blk, blk), 1)
    tri = col <= row
    key4 = lax.broadcasted_iota(jnp.int32, (blk, rows4), 0)
    qry4 = lax.broadcasted_iota(jnp.int32, (blk, rows4), 1) & (blk - 1)
    cur4 = key4 <= qry4
    sink_row4 = key4 == 0
    keycol = lax.broadcasted_iota(jnp.int32, (HEAD_DIM, 2 * blk), 1)
    scale = 1.0 / math.sqrt(HEAD_DIM)

    def modulated_norm():
        shift = mod_ref[pl.ds(b_in, 1), 0:d_model]
        gain = ng_ref[...] * (1.0 + mod_ref[pl.ds(b_in, 1), d_model:2 * d_model])
        for r in range(0, t, blk):
            x = x_ref[r:r + blk, :]
            ms = jnp.mean(x * x, axis=-1, keepdims=True)
            h_ref[r:r + blk, :] = ((x * lax.rsqrt(ms + EPS)) * gain + shift).astype(BF16)

    def in_panels(first, count):
        for n in range(first, first + count):
            cs = slice(n * MXU_WIDTH, (n + 1) * MXU_WIDTH)
            z_ref[:, cs] = jnp.dot(h_ref[...], win_ref[:, cs], preferred_element_type=F32).astype(BF16)

    def out_panels(first, count):
        for n in range(first, first + count):
            cs = slice(n * MXU_WIDTH, (n + 1) * MXU_WIDTH)
            y_ref[:, cs] = jnp.dot(a_ref[...], wout_ref[:, cs], preferred_element_type=F32)

    def finish(j):
        rows = slice(j * blk, (j + 1) * blk)
        xn = xp_ref[rows, :] + mod_ref[pl.ds(b_out, 1), 2 * d_model:3 * d_model] * y_ref[rows, :]
        ms = jnp.mean(xn * xn, axis=-1, keepdims=True)
        o_ref[rows, :] = (xn * lax.rsqrt(ms + EPS)) * fg_ref[...]

    kv_cache = {}

    def block_kv(j):
        if j in kv_cache:
            return kv_cache[j]
        r0 = j * blk
        rows = slice(r0, r0 + blk)
        k_cur = z_ref[rows, off_k:off_k + d_kv]
        v_cur = z_ref[rows, off_v:off_v + d_kv]
        if j == 0:
            k_prev = kvp_ref[:, 0:d_kv]
            v_prev = kvp_ref[:, d_kv:2 * d_kv]
        else:
            k_prev = z_ref[r0 - blk:r0, off_k:off_k + d_kv]
            v_prev = z_ref[r0 - blk:r0, off_v:off_v + d_kv]
        kext = jnp.concatenate([k_prev, k_cur], axis=0).astype(F32) * scale
        kroll = pltpu.roll(kext, HEAD_DIM, 1)
        zero2 = jnp.zeros_like(kext)
        k_lo = [jnp.where(lo2, kext, zero2), jnp.where(lo2, kroll, zero2)]
        k_hi = [jnp.where(lo2, zero2, kroll), jnp.where(lo2, zero2, kext)]
        vext_t = jnp.concatenate([v_prev, v_cur], axis=0).astype(F32).T
        kv_cache[j] = (k_lo, k_hi, vext_t)
        return kv_cache[j]

    def attention(j, kv):
        rows = slice(j * blk, (j + 1) * blk)
        k_lo, k_hi, vext_t = block_kv(j)
        p0 = kv * pairs_per_kv
        kbd = jnp.concatenate([k_lo[kv], k_hi[kv]], axis=0).astype(BF16)
        q4 = jnp.concatenate(
            [z_ref[rows, (p0 + p) * LANES:(p0 + p + 1) * LANES] for p in range(pairs_per_kv)],
            axis=0)
        s_t = lax.dot_general(kbd, q4, (((1,), (1,)), ((), ())),
                              preferred_element_type=F32)
        yield
        v_t = vext_t[kv * HEAD_DIM:(kv + 1) * HEAD_DIM, :]
        v_t = jnp.where(keycol == 0, 0.0, v_t)
        v_aug = jnp.concatenate([v_t, jnp.ones((16, 2 * blk), F32)], axis=0).astype(BF16)

        pm = []
        for e in range(2):
            s_prev = s_t[2 * blk * e:2 * blk * e + blk, :]
            s_cur = s_t[2 * blk * e + blk:2 * blk * (e + 1), :]
            if j == 0:
                s_prev = s_prev + pen0
            sc = jnp.where(cur4, s_cur, s_prev)
            sink = jnp.concatenate(
                [jnp.full((1, blk), sinks_ref[2 * (p0 + p) + e], F32)
                 for p in range(pairs_per_kv)], axis=1)
            m = jnp.maximum(jnp.max(sc, axis=0, keepdims=True), sink)
            pr = jnp.exp(sc - m)
            p_sink = jnp.exp(sink - m)
            zero4 = jnp.zeros_like(pr)
            pm_prev = jnp.where(sink_row4, p_sink, jnp.where(cur4, zero4, pr))
            pm_cur = jnp.where(cur4, pr, zero4)
            pm.append(jnp.concatenate([pm_prev, pm_cur], axis=0).astype(BF16))

        att = []
        for p in range(pairs_per_kv):
            cs = slice(p * blk, (p + 1) * blk)
            pm_pair = jnp.concatenate([pm[0][:, cs], pm[1][:, cs]], axis=1)
            o_t = jnp.dot(v_aug, pm_pair, preferred_element_type=F32)
            inv = 1.0 / o_t[HEAD_DIM:HEAD_DIM + 8, :]
            inv = jnp.concatenate([inv] * (HEAD_DIM // 8), axis=0)
            att_t = o_t[0:HEAD_DIM, :] * inv
            att.append(jnp.concatenate([att_t[:, 0:blk], att_t[:, blk:2 * blk]], axis=0).T)
        yield
        for p in range(pairs_per_kv):
            c0 = (p0 + p) * LANES
            g = z_ref[rows, off_ga + c0:off_ga + c0 + LANES].astype(F32)
            a_ref[rows, c0:c0 + LANES] = (att[p] * _silu(g)).astype(BF16)

    def spatial_gating(j):
        rows = slice(j * blk, (j + 1) * blk)
        vs = z_ref[rows, off_vs:off_vs + d_sgu].astype(F32)
        mu = jnp.mean(vs, axis=-1, keepdims=True)
        dv = vs - mu
        var = jnp.mean(dv * dv, axis=-1, keepdims=True)
        vn = ((dv * lax.rsqrt(var + EPS)) * lng_ref[...] + lnb_ref[...]).astype(BF16)
        yield
        mixed = []
        for g in range(SGU_GROUPS):
            c0 = g * LANES
            w = jnp.where(tri, sw_ref[g], 0.0).astype(BF16)
            mixed.append(jnp.dot(w, vn[:, c0:c0 + LANES], preferred_element_type=F32) + sbt_ref[:, g:g + 1])
        yield
        for g in range(SGU_GROUPS):
            c0 = g * LANES
            u = z_ref[rows, off_u + c0:off_u + c0 + LANES].astype(F32)
            gg = z_ref[rows, off_gs + c0:off_gs + c0 + LANES].astype(F32)
            a_ref[rows, d_attn + c0:d_attn + c0 + LANES] = ((u * mixed[g]) * _silu(gg)).astype(BF16)

    assert n_blk == 2 and N_KV_HEADS == 2, "the emission schedule below is written for two blocks per tile"
    panel = {name: (off // MXU_WIDTH, width // MXU_WIDTH) for name, off, width in (
        ("q", 0, d_attn), ("kv", off_k, 2 * d_kv), ("ga", off_ga, d_attn),
        ("u", off_u, d_sgu), ("vs", off_vs, d_sgu), ("gs", off_gs, d_sgu))}
    assert all(c >= 1 for _, c in panel.values()) and d_in % MXU_WIDTH == 0

    n_out = d_model // MXU_WIDTH

    @pl.when(s_id < n_tiles)
    def _():
        kvp_ref[...] = z_ref[t - blk:t, off_k:off_k + 2 * d_kv]
        rest = [n for name in ("vs", "ga", "u", "gs") for n in range(panel[name][0], sum(panel[name]))]

        def fill(count):
            for _ in range(count):
                in_panels(rest.pop(0), 1)

        out_panels(0, n_out // 2)
        modulated_norm()
        in_panels(*panel["q"])
        in_panels(*panel["kv"])
        fill(1)
        att = [attention(j, kv) for j in range(n_blk) for kv in range(N_KV_HEADS)]
        sgu = [spatial_gating(j) for j in range(n_blk)]
        for i, piece in enumerate(att):
            next(piece)
            if i == 0:
                out_panels(n_out // 2, n_out - n_out // 2)
            else:
                fill(4)
            next(piece)
            if i == 0:
                for j in range(n_blk):
                    finish(j)
        for piece in sgu:
            next(piece)
        fill(2)
        for piece in att:
            for _ in piece:
                pass
        for piece in sgu:
            next(piece)
        fill(len(rest))
        for piece in sgu:
            for _ in piece:
                pass

    @pl.when(s_id == n_tiles)
    def _():
        out_panels(0, n_out)
        for j in range(n_blk):
            finish(j)


def _layer(sinks, x2, c, w_ada, b_ada, norm_g, w_in, ln_g, ln_b, sgu_w, sgu_bt, w_out, final_g, seq, d_attn, d_sgu):
    m, d = x2.shape
    t = TILE_TOKENS
    d_in = w_in.shape[1]
    d_mix = w_out.shape[0]
    d_kv = N_KV_HEADS * HEAD_DIM
    n_tiles = m // t
    mod_rows = 8
    assert c.shape[0] <= mod_rows

    def in_tile(s):
        return jnp.minimum(s, n_tiles - 1)

    def out_tile(s):
        return jnp.maximum(s - 1, 0)

    const2 = lambda s: (0, 0)
    const3 = lambda s: (0, 0, 0)
    return pl.pallas_call(
        functools.partial(_layer_kernel, d_attn=d_attn, d_sgu=d_sgu, steps_per_seq=seq // t, n_tiles=n_tiles),
        grid=(n_tiles + 1,),
        in_specs=[
            pl.BlockSpec(memory_space=pltpu.SMEM),
            pl.BlockSpec((t, d), lambda s: (in_tile(s), 0)),
            pl.BlockSpec((t, d), lambda s: (out_tile(s), 0)),
            pl.BlockSpec(c.shape, const2),
            pl.BlockSpec(memory_space=pl.ANY),
            pl.BlockSpec(b_ada.shape, const2),
            pl.BlockSpec((1, d), const2),
            pl.BlockSpec(memory_space=pl.ANY),
            pl.BlockSpec((1, d_sgu), const2),
            pl.BlockSpec((1, d_sgu), const2),
            pl.BlockSpec((SGU_GROUPS, WINDOW, WINDOW), const3),
            pl.BlockSpec((WINDOW, SGU_GROUPS), const2),
            pl.BlockSpec(memory_space=pl.ANY),
            pl.BlockSpec((1, d), const2),
        ],
        out_specs=pl.BlockSpec((t, d), lambda s: (out_tile(s), 0)),
        out_shape=jax.ShapeDtypeStruct((m, d), F32),
        scratch_shapes=[
            pltpu.VMEM((mod_rows, w_ada.shape[1]), F32),
            pltpu.VMEM((d, d_in), BF16),
            pltpu.VMEM((d_mix, d + LANES), BF16),
            pltpu.VMEM((STAGE_SLOTS, STAGE_ROWS, STAGE_COLS), F32),
            pltpu.SemaphoreType.DMA((STAGE_SLOTS,)),
            pltpu.VMEM((t, d), BF16),
            pltpu.VMEM((t, d_in), BF16),
            pltpu.VMEM((WINDOW, 2 * d_kv), BF16),
            pltpu.VMEM((t, d_mix), BF16),
            pltpu.VMEM((t, d), F32),
        ],
        compiler_params=pltpu.CompilerParams(
            dimension_semantics=("arbitrary",), vmem_limit_bytes=VMEM_LIMIT_BYTES),
        name="layer",
    )(sinks, x2, x2, c, w_ada, b_ada, norm_g, w_in, ln_g, ln_b, sgu_w, sgu_bt, w_out, final_g)


def kernel(x, c, norm_g, w_ada, b_ada, w_in, attn_sinks, sgu_ln_g, sgu_ln_b, sgu_w, sgu_b, w_out, final_g):
    batch, seq, d = x.shape
    depth = norm_g.shape[0]
    n_q_heads = attn_sinks.shape[1]
    d_attn = n_q_heads * HEAD_DIM
    d_sgu = sgu_ln_g.shape[1]
    assert sgu_w.shape[1:] == (SGU_GROUPS, WINDOW, WINDOW) and d_sgu == SGU_GROUPS * LANES
    assert depth == 1, "stacked layers need the un-normalised residual stream between layers"

    x2 = x.reshape(batch * seq, d)
    out = _layer(attn_sinks[0], x2, c, w_ada[0], b_ada[0].reshape(1, -1), norm_g[0].reshape(1, d), w_in[0],
                 sgu_ln_g[0].reshape(1, d_sgu), sgu_ln_b[0].reshape(1, d_sgu), sgu_w[0], sgu_b[0].T,
                 w_out[0], final_g.reshape(1, d), seq, d_attn, d_sgu)
    return out.reshape(batch, seq, d)
```

```python
import functools
import math

import jax
import jax.numpy as jnp
from jax import lax
from jax.experimental import pallas as pl
from jax.experimental.pallas import tpu as pltpu

HEAD_DIM = 64
N_KV_HEADS = 2
WINDOW = 128
SGU_GROUPS = 8
EPS = 1e-6

F32 = jnp.float32
BF16 = jnp.bfloat16

LANES = 128
MXU_WIDTH = 256
VMEM_LIMIT_BYTES = 60 * 1024 * 1024
TILE_TOKENS = 256
STAGE_ROWS = 128
STAGE_COLS = 3072
STAGE_SLOTS = 3


def _silu(v):
    return v / (1.0 + jnp.exp(-v))


def _stream_weight(w_hbm, stage_ref, sem_ref, consume):
    k, n = w_hbm.shape
    col_parts = pl.cdiv(n, STAGE_COLS)
    width = n // col_parts
    assert k % STAGE_ROWS == 0 and n % col_parts == 0 and width % LANES == 0
    pieces = [(i, p) for i in range(k // STAGE_ROWS) for p in range(col_parts)]

    def copy(j):
        i, p = pieces[j]
        slot = j % STAGE_SLOTS
        return pltpu.make_async_copy(
            w_hbm.at[pl.ds(i * STAGE_ROWS, STAGE_ROWS), pl.ds(p * width, width)],
            stage_ref.at[slot, :, pl.ds(0, width)], sem_ref.at[slot])

    ahead = STAGE_SLOTS - 1
    for j in range(min(ahead, len(pieces))):
        copy(j).start()
    for j, (i, p) in enumerate(pieces):
        if j + ahead < len(pieces):
            copy(j + ahead).start()
        copy(j).wait()
        consume(slice(i * STAGE_ROWS, (i + 1) * STAGE_ROWS), slice(p * width, (p + 1) * width),
                stage_ref[j % STAGE_SLOTS, :, 0:width])


def _layer_kernel(sinks_ref, x_ref, xp_ref, c_ref, wada_hbm, bada_ref, ng_ref, win_hbm, lng_ref, lnb_ref,
                  sw_ref, sbt_ref, wout_hbm, fg_ref, o_ref,
                  mod_ref, win_ref, wout_ref, stage_ref, sem_ref, h_ref, z_ref, kvp_ref, a_ref, y_ref,
                  *, d_attn, d_sgu, steps_per_seq, n_tiles):
    t, d_model = x_ref.shape
    s_id = pl.program_id(0)
    blk = WINDOW
    n_blk = t // blk
    d_kv = N_KV_HEADS * HEAD_DIM
    pairs_per_kv = d_attn // LANES // N_KV_HEADS
    rows4 = pairs_per_kv * blk

    off_k = d_attn
    off_v = off_k + d_kv
    off_ga = off_v + d_kv
    off_u = off_ga + d_attn
    off_vs = off_u + d_sgu
    off_gs = off_vs + d_sgu
    d_in = off_gs + d_sgu

    b_in = jnp.minimum(s_id, n_tiles - 1) // steps_per_seq
    b_out = jnp.maximum(s_id - 1, 0) // steps_per_seq
    first_in_seq = lax.rem(s_id, steps_per_seq) == 0
    pen0 = jnp.where(first_in_seq, -jnp.inf, 0.0).astype(F32)

    @pl.when(s_id == 0)
    def _():
        a_ref[...] = jnp.zeros_like(a_ref)
        z_ref[t - blk:t, off_k:off_k + 2 * d_kv] = jnp.zeros((blk, 2 * d_kv), BF16)

        batch = c_ref.shape[0]
        c_act = _silu(c_ref[...])
        c_act = jnp.concatenate([c_act, jnp.zeros((mod_ref.shape[0] - batch, d_model), F32)], axis=0).astype(BF16)
        mod_ref[...] = jnp.broadcast_to(bada_ref[...], mod_ref.shape)

        def ada_piece(rows, cols, piece):
            mod_ref[:, cols] += jnp.dot(c_act[:, rows], piece.astype(BF16), preferred_element_type=F32)

        def win_piece(rows, cols, piece):
            win_ref[rows, cols] = piece.astype(BF16)

        def wout_piece(rows, cols, piece):
            wout_ref[rows, cols] = piece.astype(BF16)

        _stream_weight(wada_hbm, stage_ref, sem_ref, ada_piece)
        _stream_weight(win_hbm, stage_ref, sem_ref, win_piece)
        _stream_weight(wout_hbm, stage_ref, sem_ref, wout_piece)

    lane2 = lax.broadcasted_iota(jnp.int32, (2 * blk, LANES), 1)
    lo2 = lane2 < HEAD_DIM
    row = lax.broadcasted_iota(jnp.int32, (blk, blk), 0)
    col = lax.broadcasted_iota(jnp.int32, (blk, blk), 1)
    tri = col <= row
    key4 = lax.broadcasted_iota(jnp.int32, (blk, rows4), 0)
    qry4 = lax.broadcasted_iota(jnp.int32, (blk, rows4), 1) & (blk - 1)
    cur4 = key4 <= qry4
    sink_row4 = key4 == 0
    keycol = lax.broadcasted_iota(jnp.int32, (HEAD_DIM, 2 * blk), 1)
    scale = 1.0 / math.sqrt(HEAD_DIM)

    def modulated_norm():
        shift = mod_ref[pl.ds(b_in, 1), 0:d_model]
        gain = ng_ref[...] * (1.0 + mod_ref[pl.ds(b_in, 1), d_model:2 * d_model])
        for r in range(0, t, blk):
            x = x_ref[r:r + blk, :]
            ms = jnp.mean(x * x, axis=-1, keepdims=True)
            h_ref[r:r + blk, :] = ((x * lax.rsqrt(ms + EPS)) * gain + shift).astype(BF16)

    def in_panels(first, count):
        for n in range(first, first + count):
            cs = slice(n * MXU_WIDTH, (n + 1) * MXU_WIDTH)
            z_ref[:, cs] = jnp.dot(h_ref[...], win_ref[:, cs], preferred_element_type=F32).astype(BF16)

    def out_panels(first, count):
        for n in range(first, first + count):
            cs = slice(n * MXU_WIDTH, (n + 1) * MXU_WIDTH)
            y_ref[:, cs] = jnp.dot(a_ref[...], wout_ref[:, cs], preferred_element_type=F32)

    def finish(j):
        rows = slice(j * blk, (j + 1) * blk)
        xn = xp_ref[rows, :] + mod_ref[pl.ds(b_out, 1), 2 * d_model:3 * d_model] * y_ref[rows, :]
        ms = jnp.mean(xn * xn, axis=-1, keepdims=True)
        o_ref[rows, :] = (xn * lax.rsqrt(ms + EPS)) * fg_ref[...]

    kv_cache = {}

    def block_kv(j):
        if j in kv_cache:
            return kv_cache[j]
        r0 = j * blk
        rows = slice(r0, r0 + blk)
        k_cur = z_ref[rows, off_k:off_k + d_kv]
        v_cur = z_ref[rows, off_v:off_v + d_kv]
        if j == 0:
            k_prev = kvp_ref[:, 0:d_kv]
            v_prev = kvp_ref[:, d_kv:2 * d_kv]
        else:
            k_prev = z_ref[r0 - blk:r0, off_k:off_k + d_kv]
            v_prev = z_ref[r0 - blk:r0, off_v:off_v + d_kv]
        kext = jnp.concatenate([k_prev, k_cur], axis=0).astype(F32) * scale
        kroll = pltpu.roll(kext, HEAD_DIM, 1)
        zero2 = jnp.zeros_like(kext)
        k_lo = [jnp.where(lo2, kext, zero2), jnp.where(lo2, kroll, zero2)]
        k_hi = [jnp.where(lo2, zero2, kroll), jnp.where(lo2, zero2, kext)]
        vext_t = jnp.concatenate([v_prev, v_cur], axis=0).astype(F32).T
        kv_cache[j] = (k_lo, k_hi, vext_t)
        return kv_cache[j]

    def attention(j, kv):
        rows = slice(j * blk, (j + 1) * blk)
        k_lo, k_hi, vext_t = block_kv(j)
        p0 = kv * pairs_per_kv
        kbd = jnp.concatenate([k_lo[kv], k_hi[kv]], axis=0).astype(BF16)
        q4 = jnp.concatenate(
            [z_ref[rows, (p0 + p) * LANES:(p0 + p + 1) * LANES] for p in range(pairs_per_kv)],
            axis=0)
        s_t = lax.dot_general(kbd, q4, (((1,), (1,)), ((), ())),
                              preferred_element_type=F32)
        yield
        v_t = vext_t[kv * HEAD_DIM:(kv + 1) * HEAD_DIM, :]
        v_t = jnp.where(keycol == 0, 0.0, v_t)
        v_aug = jnp.concatenate([v_t, jnp.ones((16, 2 * blk), F32)], axis=0).astype(BF16)

        pm = []
        for e in range(2):
            s_prev = s_t[2 * blk * e:2 * blk * e + blk, :]
            s_cur = s_t[2 * blk * e + blk:2 * blk * (e + 1), :]
            if j == 0:
                s_prev = s_prev + pen0
            sc = jnp.where(cur4, s_cur, s_prev)
            sink = jnp.concatenate(
                [jnp.full((1, blk), sinks_ref[2 * (p0 + p) + e], F32)
                 for p in range(pairs_per_kv)], axis=1)
            m = jnp.maximum(jnp.max(sc, axis=0, keepdims=True), sink)
            pr = jnp.exp(sc - m)
            p_sink = jnp.exp(sink - m)
            zero4 = jnp.zeros_like(pr)
            pm_prev = jnp.where(sink_row4, p_sink, jnp.where(cur4, zero4, pr))
            pm_cur = jnp.where(cur4, pr, zero4)
            pm.append(jnp.concatenate([pm_prev, pm_cur], axis=0).astype(BF16))

        att = []
        for p in range(pairs_per_kv):
            cs = slice(p * blk, (p + 1) * blk)
            pm_pair = jnp.concatenate([pm[0][:, cs], pm[1][:, cs]], axis=1)
            o_t = jnp.dot(v_aug, pm_pair, preferred_element_type=F32)
            inv = 1.0 / o_t[HEAD_DIM:HEAD_DIM + 8, :]
            inv = jnp.concatenate([inv] * (HEAD_DIM // 8), axis=0)
            att_t = o_t[0:HEAD_DIM, :] * inv
            att.append(jnp.concatenate([att_t[:, 0:blk], att_t[:, blk:2 * blk]], axis=0).T)
        yield
        for p in range(pairs_per_kv):
            c0 = (p0 + p) * LANES
            g = z_ref[rows, off_ga + c0:off_ga + c0 + LANES].astype(F32)
            a_ref[rows, c0:c0 + LANES] = (att[p] * _silu(g)).astype(BF16)

    def spatial_gating(j):
        rows = slice(j * blk, (j + 1) * blk)
        vs = z_ref[rows, off_vs:off_vs + d_sgu].astype(F32)
        mu = jnp.mean(vs, axis=-1, keepdims=True)
        dv = vs - mu
        var = jnp.mean(dv * dv, axis=-1, keepdims=True)
        vn = ((dv * lax.rsqrt(var + EPS)) * lng_ref[...] + lnb_ref[...]).astype(BF16)
        yield
        mixed = []
        for g in range(SGU_GROUPS):
            c0 = g * LANES
            w = jnp.where(tri, sw_ref[g], 0.0).astype(BF16)
            mixed.append(jnp.dot(w, vn[:, c0:c0 + LANES], preferred_element_type=F32) + sbt_ref[:, g:g + 1])
        yield
        for g in range(SGU_GROUPS):
            c0 = g * LANES
            u = z_ref[rows, off_u + c0:off_u + c0 + LANES].astype(F32)
            gg = z_ref[rows, off_gs + c0:off_gs + c0 + LANES].astype(F32)
            a_ref[rows, d_attn + c0:d_attn + c0 + LANES] = ((u * mixed[g]) * _silu(gg)).astype(BF16)

    assert n_blk == 2 and N_KV_HEADS == 2, "the emission schedule below is written for two blocks per tile"
    panel = {name: (off // MXU_WIDTH, width // MXU_WIDTH) for name, off, width in (
        ("q", 0, d_attn), ("kv", off_k, 2 * d_kv), ("ga", off_ga, d_attn),
        ("u", off_u, d_sgu), ("vs", off_vs, d_sgu), ("gs", off_gs, d_sgu))}
    assert all(c >= 1 for _, c in panel.values()) and d_in % MXU_WIDTH == 0

    n_out = d_model // MXU_WIDTH

    @pl.when(s_id < n_tiles)
    def _():
        kvp_ref[...] = z_ref[t - blk:t, off_k:off_k + 2 * d_kv]
        rest = [n for name in ("vs", "ga", "u", "gs") for n in range(panel[name][0], sum(panel[name]))]

        def fill(count):
            for _ in range(count):
                in_panels(rest.pop(0), 1)

        out_panels(0, n_out // 2)
        modulated_norm()
        in_panels(*panel["q"])
        in_panels(*panel["kv"])
        fill(1)
        att = [attention(j, kv) for j in range(n_blk) for kv in range(N_KV_HEADS)]
        sgu = [spatial_gating(j) for j in range(n_blk)]
        for i, piece in enumerate(att):
            next(piece)
            if i == 0:
                out_panels(n_out // 2, n_out - n_out // 2)
            else:
                fill(4)
            next(piece)
            if i == 0:
                for j in range(n_blk):
                    finish(j)
        for piece in sgu:
            next(piece)
        fill(2)
        for piece in att:
            for _ in piece:
                pass
        for piece in sgu:
            next(piece)
        fill(len(rest))
        for piece in sgu:
            for _ in piece:
                pass

    @pl.when(s_id == n_tiles)
    def _():
        out_panels(0, n_out)
        for j in range(n_blk):
            finish(j)


def _layer(sinks, x2, c, w_ada, b_ada, norm_g, w_in, ln_g, ln_b, sgu_w, sgu_bt, w_out, final_g, seq, d_attn, d_sgu):
    m, d = x2.shape
    t = TILE_TOKENS
    d_in = w_in.shape[1]
    d_mix = w_out.shape[0]
    d_kv = N_KV_HEADS * HEAD_DIM
    n_tiles = m // t
    mod_rows = 8
    assert c.shape[0] <= mod_rows

    def in_tile(s):
        return jnp.minimum(s, n_tiles - 1)

    def out_tile(s):
        return jnp.maximum(s - 1, 0)

    const2 = lambda s: (0, 0)
    const3 = lambda s: (0, 0, 0)
    return pl.pallas_call(
        functools.partial(_layer_kernel, d_attn=d_attn, d_sgu=d_sgu, steps_per_seq=seq // t, n_tiles=n_tiles),
        grid=(n_tiles + 1,),
        in_specs=[
            pl.BlockSpec(memory_space=pltpu.SMEM),
            pl.BlockSpec((t, d), lambda s: (in_tile(s), 0)),
            pl.BlockSpec((t, d), lambda s: (out_tile(s), 0)),
            pl.BlockSpec(c.shape, const2),
            pl.BlockSpec(memory_space=pl.ANY),
            pl.BlockSpec(b_ada.shape, const2),
            pl.BlockSpec((1, d), const2),
            pl.BlockSpec(memory_space=pl.ANY),
            pl.BlockSpec((1, d_sgu), const2),
            pl.BlockSpec((1, d_sgu), const2),
            pl.BlockSpec((SGU_GROUPS, WINDOW, WINDOW), const3),
            pl.BlockSpec((WINDOW, SGU_GROUPS), const2),
            pl.BlockSpec(memory_space=pl.ANY),
            pl.BlockSpec((1, d), const2),
        ],
        out_specs=pl.BlockSpec((t, d), lambda s: (out_tile(s), 0)),
        out_shape=jax.ShapeDtypeStruct((m, d), F32),
        scratch_shapes=[
            pltpu.VMEM((mod_rows, w_ada.shape[1]), F32),
            pltpu.VMEM((d, d_in), BF16),
            pltpu.VMEM((d_mix, d + LANES), BF16),
            pltpu.VMEM((STAGE_SLOTS, STAGE_ROWS, STAGE_COLS), F32),
            pltpu.SemaphoreType.DMA((STAGE_SLOTS,)),
            pltpu.VMEM((t, d), BF16),
            pltpu.VMEM((t, d_in), BF16),
            pltpu.VMEM((WINDOW, 2 * d_kv), BF16),
            pltpu.VMEM((t, d_mix), BF16),
            pltpu.VMEM((t, d), F32),
        ],
        compiler_params=pltpu.CompilerParams(
            dimension_semantics=("arbitrary",), vmem_limit_bytes=VMEM_LIMIT_BYTES),
        name="layer",
    )(sinks, x2, x2, c, w_ada, b_ada, norm_g, w_in, ln_g, ln_b, sgu_w, sgu_bt, w_out, final_g)


def kernel(x, c, norm_g, w_ada, b_ada, w_in, attn_sinks, sgu_ln_g, sgu_ln_b, sgu_w, sgu_b, w_out, final_g):
    batch, seq, d = x.shape
    depth = norm_g.shape[0]
    n_q_heads = attn_sinks.shape[1]
    d_attn = n_q_heads * HEAD_DIM
    d_sgu = sgu_ln_g.shape[1]
    assert sgu_w.shape[1:] == (SGU_GROUPS, WINDOW, WINDOW) and d_sgu == SGU_GROUPS * LANES
    assert depth == 1, "stacked layers need the un-normalised residual stream between layers"

    x2 = x.reshape(batch * seq, d)
    out = _layer(attn_sinks[0], x2, c, w_ada[0], b_ada[0].reshape(1, -1), norm_g[0].reshape(1, d), w_in[0],
                 sgu_ln_g[0].reshape(1, d_sgu), sgu_ln_b[0].reshape(1, d_sgu), sgu_w[0], sgu_b[0].T,
                 w_out[0], final_g.reshape(1, d), seq, d_attn, d_sgu)
    return out.reshape(batch, seq, d)
```

```python
import functools
import math

import jax
import jax.numpy as jnp
from jax import lax
from jax.experimental import pallas as pl
from jax.experimental.pallas import tpu as pltpu

HEAD_DIM = 64
N_KV_HEADS = 2
WINDOW = 128
SGU_GROUPS = 8
EPS = 1e-6

F32 = jnp.float32
BF16 = jnp.bfloat16

LANES = 128
MXU_WIDTH = 256
VMEM_LIMIT_BYTES = 60 * 1024 * 1024
TILE_TOKENS = 256
STAGE_ROWS = 128
STAGE_COLS = 3072
STAGE_SLOTS = 4


def _silu(v):
    return v / (1.0 + jnp.exp(-v))


def _stream_weight(w_hbm, stage_ref, sem_ref, consume):
    k, n = w_hbm.shape
    col_parts = pl.cdiv(n, STAGE_COLS)
    width = n // col_parts
    assert k % STAGE_ROWS == 0 and n % col_parts == 0 and width % LANES == 0
    pieces = [(i, p) for i in range(k // STAGE_ROWS) for p in range(col_parts)]

    def copy(j):
        i, p = pieces[j]
        slot = j % STAGE_SLOTS
        return pltpu.make_async_copy(
            w_hbm.at[pl.ds(i * STAGE_ROWS, STAGE_ROWS), pl.ds(p * width, width)],
            stage_ref.at[slot, :, pl.ds(0, width)], sem_ref.at[slot])

    ahead = STAGE_SLOTS - 1
    for j in range(min(ahead, len(pieces))):
        copy(j).start()
    for j, (i, p) in enumerate(pieces):
        if j + ahead < len(pieces):
            copy(j + ahead).start()
        copy(j).wait()
        consume(slice(i * STAGE_ROWS, (i + 1) * STAGE_ROWS), slice(p * width, (p + 1) * width),
                stage_ref[j % STAGE_SLOTS, :, 0:width])


def _layer_kernel(sinks_ref, x_ref, xp_ref, c_ref, wada_hbm, bada_ref, ng_ref, win_hbm, lng_ref, lnb_ref,
                  sw_ref, sbt_ref, wout_hbm, fg_ref, o_ref,
                  mod_ref, win_ref, wout_ref, stage_ref, sem_ref, h_ref, z_ref, kvp_ref, a_ref, y_ref,
                  *, d_attn, d_sgu, steps_per_seq, n_tiles):
    t, d_model = x_ref.shape
    s_id = pl.program_id(0)
    blk = WINDOW
    n_blk = t // blk
    d_kv = N_KV_HEADS * HEAD_DIM
    pairs_per_kv = d_attn // LANES // N_KV_HEADS
    rows4 = pairs_per_kv * blk

    off_k = d_attn
    off_v = off_k + d_kv
    off_ga = off_v + d_kv
    off_u = off_ga + d_attn
    off_vs = off_u + d_sgu
    off_gs = off_vs + d_sgu
    d_in = off_gs + d_sgu

    b_in = jnp.minimum(s_id, n_tiles - 1) // steps_per_seq
    b_out = jnp.maximum(s_id - 1, 0) // steps_per_seq
    first_in_seq = lax.rem(s_id, steps_per_seq) == 0
    pen0 = jnp.where(first_in_seq, -jnp.inf, 0.0).astype(F32)

    @pl.when(s_id == 0)
    def _():
        a_ref[...] = jnp.zeros_like(a_ref)
        z_ref[t - blk:t, off_k:off_k + 2 * d_kv] = jnp.zeros((blk, 2 * d_kv), BF16)

        batch = c_ref.shape[0]
        c_act = _silu(c_ref[...])
        c_act = jnp.concatenate([c_act, jnp.zeros((mod_ref.shape[0] - batch, d_model), F32)], axis=0).astype(BF16)
        mod_ref[...] = jnp.broadcast_to(bada_ref[...], mod_ref.shape)

        def ada_piece(rows, cols, piece):
            mod_ref[:, cols] += jnp.dot(c_act[:, rows], piece.astype(BF16), preferred_element_type=F32)

        def win_piece(rows, cols, piece):
            win_ref[rows, cols] = piece.astype(BF16)

        def wout_piece(rows, cols, piece):
            wout_ref[rows, cols] = piece.astype(BF16)

        _stream_weight(wada_hbm, stage_ref, sem_ref, ada_piece)
        _stream_weight(win_hbm, stage_ref, sem_ref, win_piece)
        _stream_weight(wout_hbm, stage_ref, sem_ref, wout_piece)

    lane2 = lax.broadcasted_iota(jnp.int32, (2 * blk, LANES), 1)
    lo2 = lane2 < HEAD_DIM
    row = lax.broadcasted_iota(jnp.int32, (blk, blk), 0)
    col = lax.broadcasted_iota(jnp.int32, (blk, blk), 1)
    tri = col <= row
    key4 = lax.broadcasted_iota(jnp.int32, (blk, rows4), 0)
    qry4 = lax.broadcasted_iota(jnp.int32, (blk, rows4), 1) & (blk - 1)
    cur4 = key4 <= qry4
    sink_row4 = key4 == 0
    keycol = lax.broadcasted_iota(jnp.int32, (HEAD_DIM, 2 * blk), 1)
    scale = 1.0 / math.sqrt(HEAD_DIM)

    def modulated_norm():
        shift = mod_ref[pl.ds(b_in, 1), 0:d_model]
        gain = ng_ref[...] * (1.0 + mod_ref[pl.ds(b_in, 1), d_model:2 * d_model])
        for r in range(0, t, blk):
            x = x_ref[r:r + blk, :]
            ms = jnp.mean(x * x, axis=-1, keepdims=True)
            h_ref[r:r + blk, :] = ((x * lax.rsqrt(ms + EPS)) * gain + shift).astype(BF16)

    def in_panels(first, count):
        for n in range(first, first + count):
            cs = slice(n * MXU_WIDTH, (n + 1) * MXU_WIDTH)
            z_ref[:, cs] = jnp.dot(h_ref[...], win_ref[:, cs], preferred_element_type=F32).astype(BF16)

    def out_panels(first, count):
        for n in range(first, first + count):
            cs = slice(n * MXU_WIDTH, (n + 1) * MXU_WIDTH)
            y_ref[:, cs] = jnp.dot(a_ref[...], wout_ref[:, cs], preferred_element_type=F32)

    def finish(j):
        rows = slice(j * blk, (j + 1) * blk)
        xn = xp_ref[rows, :] + mod_ref[pl.ds(b_out, 1), 2 * d_model:3 * d_model] * y_ref[rows, :]
        ms = jnp.mean(xn * xn, axis=-1, keepdims=True)
        o_ref[rows, :] = (xn * lax.rsqrt(ms + EPS)) * fg_ref[...]

    kv_cache = {}

    def block_kv(j):
        if j in kv_cache:
            return kv_cache[j]
        r0 = j * blk
        rows = slice(r0, r0 + blk)
        k_cur = z_ref[rows, off_k:off_k + d_kv]
        v_cur = z_ref[rows, off_v:off_v + d_kv]
        if j == 0:
            k_prev = kvp_ref[:, 0:d_kv]
            v_prev = kvp_ref[:, d_kv:2 * d_kv]
        else:
            k_prev = z_ref[r0 - blk:r0, off_k:off_k + d_kv]
            v_prev = z_ref[r0 - blk:r0, off_v:off_v + d_kv]
        kext = jnp.concatenate([k_prev, k_cur], axis=0).astype(F32) * scale
        kroll = pltpu.roll(kext, HEAD_DIM, 1)
        zero2 = jnp.zeros_like(kext)
        k_lo = [jnp.where(lo2, kext, zero2), jnp.where(lo2, kroll, zero2)]
        k_hi = [jnp.where(lo2, zero2, kroll), jnp.where(lo2, zero2, kext)]
        vext_t = jnp.concatenate([v_prev, v_cur], axis=0).astype(F32).T
        kv_cache[j] = (k_lo, k_hi, vext_t)
        return kv_cache[j]

    def attention(j, kv):
        rows = slice(j * blk, (j + 1) * blk)
        k_lo, k_hi, vext_t = block_kv(j)
        p0 = kv * pairs_per_kv
        kbd = jnp.concatenate([k_lo[kv], k_hi[kv]], axis=0).astype(BF16)
        q4 = jnp.concatenate(
            [z_ref[rows, (p0 + p) * LANES:(p0 + p + 1) * LANES] for p in range(pairs_per_kv)],
            axis=0)
        s_t = lax.dot_general(kbd, q4, (((1,), (1,)), ((), ())),
                              preferred_element_type=F32)
        yield
        v_t = vext_t[kv * HEAD_DIM:(kv + 1) * HEAD_DIM, :]
        v_t = jnp.where(keycol == 0, 0.0, v_t)
        v_aug = jnp.concatenate([v_t, jnp.ones((16, 2 * blk), F32)], axis=0).astype(BF16)

        pm = []
        for e in range(2):
            s_prev = s_t[2 * blk * e:2 * blk * e + blk, :]
            s_cur = s_t[2 * blk * e + blk:2 * blk * (e + 1), :]
            if j == 0:
                s_prev = s_prev + pen0
            sc = jnp.where(cur4, s_cur, s_prev)
            sink = jnp.concatenate(
                [jnp.full((1, blk), sinks_ref[2 * (p0 + p) + e], F32)
                 for p in range(pairs_per_kv)], axis=1)
            m = jnp.maximum(jnp.max(sc, axis=0, keepdims=True), sink)
            pr = jnp.exp(sc - m)
            p_sink = jnp.exp(sink - m)
            zero4 = jnp.zeros_like(pr)
            pm_prev = jnp.where(sink_row4, p_sink, jnp.where(cur4, zero4, pr))
            pm_cur = jnp.where(cur4, pr, zero4)
            pm.append(jnp.concatenate([pm_prev, pm_cur], axis=0).astype(BF16))

        att = []
        for p in range(pairs_per_kv):
            cs = slice(p * blk, (p + 1) * blk)
            pm_pair = jnp.concatenate([pm[0][:, cs], pm[1][:, cs]], axis=1)
            o_t = jnp.dot(v_aug, pm_pair, preferred_element_type=F32)
            inv = 1.0 / o_t[HEAD_DIM:HEAD_DIM + 8, :]
            inv = jnp.concatenate([inv] * (HEAD_DIM // 8), axis=0)
            att_t = o_t[0:HEAD_DIM, :] * inv
            att.append(jnp.concatenate([att_t[:, 0:blk], att_t[:, blk:2 * blk]], axis=0).T)
        yield
        for p in range(pairs_per_kv):
            c0 = (p0 + p) * LANES
            g = z_ref[rows, off_ga + c0:off_ga + c0 + LANES].astype(F32)
            a_ref[rows, c0:c0 + LANES] = (att[p] * _silu(g)).astype(BF16)

    def spatial_gating(j):
        rows = slice(j * blk, (j + 1) * blk)
        vs = z_ref[rows, off_vs:off_vs + d_sgu].astype(F32)
        mu = jnp.mean(vs, axis=-1, keepdims=True)
        dv = vs - mu
        var = jnp.mean(dv * dv, axis=-1, keepdims=True)
        vn = ((dv * lax.rsqrt(var + EPS)) * lng_ref[...] + lnb_ref[...]).astype(BF16)
        yield
        mixed = []
        for g in range(SGU_GROUPS):
            c0 = g * LANES
            w = jnp.where(tri, sw_ref[g], 0.0).astype(BF16)
            mixed.append(jnp.dot(w, vn[:, c0:c0 + LANES], preferred_element_type=F32) + sbt_ref[:, g:g + 1])
        yield
        for g in range(SGU_GROUPS):
            c0 = g * LANES
            u = z_ref[rows, off_u + c0:off_u + c0 + LANES].astype(F32)
            gg = z_ref[rows, off_gs + c0:off_gs + c0 + LANES].astype(F32)
            a_ref[rows, d_attn + c0:d_attn + c0 + LANES] = ((u * mixed[g]) * _silu(gg)).astype(BF16)

    assert n_blk == 2 and N_KV_HEADS == 2, "the emission schedule below is written for two blocks per tile"
    panel = {name: (off // MXU_WIDTH, width // MXU_WIDTH) for name, off, width in (
        ("q", 0, d_attn), ("kv", off_k, 2 * d_kv), ("ga", off_ga, d_attn),
        ("u", off_u, d_sgu), ("vs", off_vs, d_sgu), ("gs", off_gs, d_sgu))}
    assert all(c >= 1 for _, c in panel.values()) and d_in % MXU_WIDTH == 0

    n_out = d_model // MXU_WIDTH

    @pl.when(s_id < n_tiles)
    def _():
        kvp_ref[...] = z_ref[t - blk:t, off_k:off_k + 2 * d_kv]
        rest = [n for name in ("vs", "ga", "u", "gs") for n in range(panel[name][0], sum(panel[name]))]

        def fill(count):
            for _ in range(count):
                in_panels(rest.pop(0), 1)

        out_panels(0, n_out // 2)
        modulated_norm()
        in_panels(*panel["q"])
        in_panels(*panel["kv"])
        fill(1)
        att = [attention(j, kv) for j in range(n_blk) for kv in range(N_KV_HEADS)]
        sgu = [spatial_gating(j) for j in range(n_blk)]
        for i, piece in enumerate(att):
            next(piece)
            if i == 0:
                out_panels(n_out // 2, n_out - n_out // 2)
            else:
                fill(4)
            next(piece)
            if i == 0:
                for j in range(n_blk):
                    finish(j)
        for piece in sgu:
            next(piece)
        fill(2)
        for piece in att:
            for _ in piece:
                pass
        for piece in sgu:
            next(piece)
        fill(len(rest))
        for piece in sgu:
            for _ in piece:
                pass

    @pl.when(s_id == n_tiles)
    def _():
        out_panels(0, n_out)
        for j in range(n_blk):
            finish(j)


def _layer(sinks, x2, c, w_ada, b_ada, norm_g, w_in, ln_g, ln_b, sgu_w, sgu_bt, w_out, final_g, seq, d_attn, d_sgu):
    m, d = x2.shape
    t = TILE_TOKENS
    d_in = w_in.shape[1]
    d_mix = w_out.shape[0]
    d_kv = N_KV_HEADS * HEAD_DIM
    n_tiles = m // t
    mod_rows = 8
    assert c.shape[0] <= mod_rows

    def in_tile(s):
        return jnp.minimum(s, n_tiles - 1)

    def out_tile(s):
        return jnp.maximum(s - 1, 0)

    const2 = lambda s: (0, 0)
    const3 = lambda s: (0, 0, 0)
    return pl.pallas_call(
        functools.partial(_layer_kernel, d_attn=d_attn, d_sgu=d_sgu, steps_per_seq=seq // t, n_tiles=n_tiles),
        grid=(n_tiles + 1,),
        in_specs=[
            pl.BlockSpec(memory_space=pltpu.SMEM),
            pl.BlockSpec((t, d), lambda s: (in_tile(s), 0)),
            pl.BlockSpec((t, d), lambda s: (out_tile(s), 0)),
            pl.BlockSpec(c.shape, const2),
            pl.BlockSpec(memory_space=pl.ANY),
            pl.BlockSpec(b_ada.shape, const2),
            pl.BlockSpec((1, d), const2),
            pl.BlockSpec(memory_space=pl.ANY),
            pl.BlockSpec((1, d_sgu), const2),
            pl.BlockSpec((1, d_sgu), const2),
            pl.BlockSpec((SGU_GROUPS, WINDOW, WINDOW), const3),
            pl.BlockSpec((WINDOW, SGU_GROUPS), const2),
            pl.BlockSpec(memory_space=pl.ANY),
            pl.BlockSpec((1, d), const2),
        ],
        out_specs=pl.BlockSpec((t, d), lambda s: (out_tile(s), 0)),
        out_shape=jax.ShapeDtypeStruct((m, d), F32),
        scratch_shapes=[
            pltpu.VMEM((mod_rows, w_ada.shape[1]), F32),
            pltpu.VMEM((d, d_in), BF16),
            pltpu.VMEM((d_mix, d + LANES), BF16),
            pltpu.VMEM((STAGE_SLOTS, STAGE_ROWS, STAGE_COLS), F32),
            pltpu.SemaphoreType.DMA((STAGE_SLOTS,)),
            pltpu.VMEM((t, d), BF16),
            pltpu.VMEM((t, d_in), BF16),
            pltpu.VMEM((WINDOW, 2 * d_kv), BF16),
            pltpu.VMEM((t, d_mix), BF16),
            pltpu.VMEM((t, d), F32),
        ],
        compiler_params=pltpu.CompilerParams(
            dimension_semantics=("arbitrary",), vmem_limit_bytes=VMEM_LIMIT_BYTES),
        name="layer",
    )(sinks, x2, x2, c, w_ada, b_ada, norm_g, w_in, ln_g, ln_b, sgu_w, sgu_bt, w_out, final_g)


def kernel(x, c, norm_g, w_ada, b_ada, w_in, attn_sinks, sgu_ln_g, sgu_ln_b, sgu_w, sgu_b, w_out, final_g):
    batch, seq, d = x.shape
    depth = norm_g.shape[0]
    n_q_heads = attn_sinks.shape[1]
    d_attn = n_q_heads * HEAD_DIM
    d_sgu = sgu_ln_g.shape[1]
    assert sgu_w.shape[1:] == (SGU_GROUPS, WINDOW, WINDOW) and d_sgu == SGU_GROUPS * LANES
    assert depth == 1, "stacked layers need the un-normalised residual stream between layers"

    x2 = x.reshape(batch * seq, d)
    out = _layer(attn_sinks[0], x2, c, w_ada[0], b_ada[0].reshape(1, -1), norm_g[0].reshape(1, d), w_in[0],
                 sgu_ln_g[0].reshape(1, d_sgu), sgu_ln_b[0].reshape(1, d_sgu), sgu_w[0], sgu_b[0].T,
                 w_out[0], final_g.reshape(1, d), seq, d_attn, d_sgu)
    return out.reshape(batch, seq, d)
```

```python
import functools
import math

import jax
import jax.numpy as jnp
from jax import lax
from jax.experimental import pallas as pl
from jax.experimental.pallas import tpu as pltpu

HEAD_DIM = 64
N_KV_HEADS = 2
WINDOW = 128
SGU_GROUPS = 8
EPS = 1e-6

F32 = jnp.float32
BF16 = jnp.bfloat16

LANES = 128
MXU_WIDTH = 256
VMEM_LIMIT_BYTES = 60 * 1024 * 1024
TILE_TOKENS = 256
STAGE_ROWS = 128
STAGE_COLS = 2688
STAGE_SLOTS = 4


def _silu(v):
    return v / (1.0 + jnp.exp(-v))


def _stream_weight(w_hbm, stage_ref, sem_ref, consume, cols=None):
    k = w_hbm.shape[0]
    c0, c1 = cols if cols is not None else (0, w_hbm.shape[1])
    col_parts = pl.cdiv(c1 - c0, STAGE_COLS)
    width = (c1 - c0) // col_parts
    assert k % STAGE_ROWS == 0 and (c1 - c0) % col_parts == 0 and width % LANES == 0 and c0 % LANES == 0
    pieces = [(i, p) for i in range(k // STAGE_ROWS) for p in range(col_parts)]

    def copy(j):
        i, p = pieces[j]
        slot = j % STAGE_SLOTS
        return pltpu.make_async_copy(
            w_hbm.at[pl.ds(i * STAGE_ROWS, STAGE_ROWS), pl.ds(c0 + p * width, width)],
            stage_ref.at[slot, :, pl.ds(0, width)], sem_ref.at[slot])

    ahead = STAGE_SLOTS - 1
    for j in range(min(ahead, len(pieces))):
        copy(j).start()
    for j, (i, p) in enumerate(pieces):
        if j + ahead < len(pieces):
            copy(j + ahead).start()
        copy(j).wait()
        consume(slice(i * STAGE_ROWS, (i + 1) * STAGE_ROWS), slice(c0 + p * width, c0 + (p + 1) * width),
                stage_ref[j % STAGE_SLOTS, :, 0:width])
        yield


def _drain(generator):
    for _ in generator:
        pass


def _layer_kernel(sinks_ref, x_ref, xp_ref, c_ref, wada_hbm, bada_ref, ng_ref, win_hbm, lng_ref, lnb_ref,
                  sw_ref, sb_ref, wout_hbm, fg_ref, o_ref,
                  mod_ref, win_ref, wout_ref, stage_ref, sem_ref, h_ref, z_ref, kvp_ref, a_ref, y_ref,
                  *, d_attn, d_sgu, steps_per_seq, n_tiles):
    t, d_model = x_ref.shape
    s_id = pl.program_id(0)
    blk = WINDOW
    n_blk = t // blk
    d_kv = N_KV_HEADS * HEAD_DIM
    pairs_per_kv = d_attn // LANES // N_KV_HEADS
    rows4 = pairs_per_kv * blk

    off_k = d_attn
    off_v = off_k + d_kv
    off_ga = off_v + d_kv
    off_u = off_ga + d_attn
    off_vs = off_u + d_sgu
    off_gs = off_vs + d_sgu
    d_in = off_gs + d_sgu

    b_in = jnp.minimum(s_id, n_tiles - 1) // steps_per_seq
    b_out = jnp.maximum(s_id - 1, 0) // steps_per_seq
    first_in_seq = lax.rem(s_id, steps_per_seq) == 0
    pen0 = jnp.where(first_in_seq, -jnp.inf, 0.0).astype(F32)

    lane2 = lax.broadcasted_iota(jnp.int32, (2 * blk, LANES), 1)
    lo2 = lane2 < HEAD_DIM
    row = lax.broadcasted_iota(jnp.int32, (blk, blk), 0)
    col = lax.broadcasted_iota(jnp.int32, (blk, blk), 1)
    tri = col <= row
    key4 = lax.broadcasted_iota(jnp.int32, (blk, rows4), 0)
    qry4 = lax.broadcasted_iota(jnp.int32, (blk, rows4), 1) & (blk - 1)
    cur4 = key4 <= qry4
    sink_row4 = key4 == 0
    keycol = lax.broadcasted_iota(jnp.int32, (HEAD_DIM, 2 * blk), 1)
    scale = 1.0 / math.sqrt(HEAD_DIM)

    def modulated_norm():
        shift = mod_ref[pl.ds(b_in, 1), 0:d_model]
        gain = ng_ref[...] * (1.0 + mod_ref[pl.ds(b_in, 1), d_model:2 * d_model])
        for r in range(0, t, blk):
            x = x_ref[r:r + blk, :]
            ms = jnp.mean(x * x, axis=-1, keepdims=True)
            h_ref[r:r + blk, :] = ((x * lax.rsqrt(ms + EPS)) * gain + shift).astype(BF16)

    def in_panels(first, count, between=lambda: None):
        for n in range(first, first + count):
            cs = slice(n * MXU_WIDTH, (n + 1) * MXU_WIDTH)
            z_ref[:, cs] = jnp.dot(h_ref[...], win_ref[:, cs], preferred_element_type=F32).astype(BF16)
            between()

    def out_panels(first, count):
        for n in range(first, first + count):
            cs = slice(n * MXU_WIDTH, (n + 1) * MXU_WIDTH)
            y_ref[:, cs] = jnp.dot(a_ref[...], wout_ref[:, cs], preferred_element_type=F32)

    def finish(j):
        rows = slice(j * blk, (j + 1) * blk)
        xn = xp_ref[rows, :] + mod_ref[pl.ds(b_out, 1), 2 * d_model:3 * d_model] * y_ref[rows, :]
        ms = jnp.mean(xn * xn, axis=-1, keepdims=True)
        o_ref[rows, :] = (xn * lax.rsqrt(ms + EPS)) * fg_ref[...]

    kv_cache = {}

    def block_kv(j):
        if j in kv_cache:
            return kv_cache[j]
        r0 = j * blk
        rows = slice(r0, r0 + blk)
        k_cur = z_ref[rows, off_k:off_k + d_kv]
        v_cur = z_ref[rows, off_v:off_v + d_kv]
        if j == 0:
            k_prev = kvp_ref[:, 0:d_kv]
            v_prev = kvp_ref[:, d_kv:2 * d_kv]
        else:
            k_prev = z_ref[r0 - blk:r0, off_k:off_k + d_kv]
            v_prev = z_ref[r0 - blk:r0, off_v:off_v + d_kv]
        kext = jnp.concatenate([k_prev, k_cur], axis=0).astype(F32) * scale
        kroll = pltpu.roll(kext, HEAD_DIM, 1)
        zero2 = jnp.zeros_like(kext)
        k_lo = [jnp.where(lo2, kext, zero2), jnp.where(lo2, kroll, zero2)]
        k_hi = [jnp.where(lo2, zero2, kroll), jnp.where(lo2, zero2, kext)]
        vext_t = jnp.concatenate([v_prev, v_cur], axis=0).astype(F32).T
        kv_cache[j] = (k_lo, k_hi, vext_t)
        return kv_cache[j]

    def attention(j, kv):
        rows = slice(j * blk, (j + 1) * blk)
        k_lo, k_hi, vext_t = block_kv(j)
        p0 = kv * pairs_per_kv
        kbd = jnp.concatenate([k_lo[kv], k_hi[kv]], axis=0).astype(BF16)
        q4 = jnp.concatenate(
            [z_ref[rows, (p0 + p) * LANES:(p0 + p + 1) * LANES] for p in range(pairs_per_kv)],
            axis=0)
        s_t = lax.dot_general(kbd, q4, (((1,), (1,)), ((), ())),
                              preferred_element_type=F32)
        yield
        v_t = vext_t[kv * HEAD_DIM:(kv + 1) * HEAD_DIM, :]
        v_t = jnp.where(keycol == 0, 0.0, v_t)
        v_aug = jnp.concatenate([v_t, jnp.ones((16, 2 * blk), F32)], axis=0).astype(BF16)

        pm = []
        for e in range(2):
            s_prev = s_t[2 * blk * e:2 * blk * e + blk, :]
            s_cur = s_t[2 * blk * e + blk:2 * blk * (e + 1), :]
            if j == 0:
                s_prev = s_prev + pen0
            sc = jnp.where(cur4, s_cur, s_prev)
            sink = jnp.concatenate(
                [jnp.full((1, blk), sinks_ref[2 * (p0 + p) + e], F32)
                 for p in range(pairs_per_kv)], axis=1)
            m = jnp.maximum(jnp.max(sc, axis=0, keepdims=True), sink)
            pr = jnp.exp(sc - m)
            p_sink = jnp.exp(sink - m)
            zero4 = jnp.zeros_like(pr)
            pm_prev = jnp.where(sink_row4, p_sink, jnp.where(cur4, zero4, pr))
            pm_cur = jnp.where(cur4, pr, zero4)
            pm.append(jnp.concatenate([pm_prev, pm_cur], axis=0).astype(BF16))

        att = []
        for p in range(pairs_per_kv):
            cs = slice(p * blk, (p + 1) * blk)
            pm_pair = jnp.concatenate([pm[0][:, cs], pm[1][:, cs]], axis=1)
            o_t = jnp.dot(v_aug, pm_pair, preferred_element_type=F32)
            inv = 1.0 / o_t[HEAD_DIM:HEAD_DIM + 8, :]
            inv = jnp.concatenate([inv] * (HEAD_DIM // 8), axis=0)
            att_t = o_t[0:HEAD_DIM, :] * inv
            att.append(jnp.concatenate([att_t[:, 0:blk], att_t[:, blk:2 * blk]], axis=0).T)
        yield
        for p in range(pairs_per_kv):
            c0 = (p0 + p) * LANES
            g = z_ref[rows, off_ga + c0:off_ga + c0 + LANES].astype(F32)
            a_ref[rows, c0:c0 + LANES] = (att[p] * _silu(g)).astype(BF16)

    def spatial_gating():
        vn = []
        for j in range(n_blk):
            vs = z_ref[j * blk:(j + 1) * blk, off_vs:off_vs + d_sgu].astype(F32)
            mu = jnp.mean(vs, axis=-1, keepdims=True)
            dv = vs - mu
            var = jnp.mean(dv * dv, axis=-1, keepdims=True)
            vn.append(((dv * lax.rsqrt(var + EPS)) * lng_ref[...] + lnb_ref[...]).astype(BF16))
        yield
        bias_t = sb_ref[...].T
        mixed = []
        for g in range(SGU_GROUPS):
            cs = slice(g * LANES, (g + 1) * LANES)
            w = jnp.where(tri, sw_ref[g], 0.0).astype(BF16)
            vn_g = jnp.concatenate([vn[j][:, cs] for j in range(n_blk)], axis=1)
            mixed.append(jnp.dot(w, vn_g, preferred_element_type=F32) + bias_t[:, g:g + 1])
        groups_per_panel = MXU_WIDTH // LANES
        for g0 in range(0, SGU_GROUPS, groups_per_panel):
            yield
            for g in range(g0, g0 + groups_per_panel):
                c0 = g * LANES
                for j in range(n_blk):
                    rows = slice(j * blk, (j + 1) * blk)
                    u = z_ref[rows, off_u + c0:off_u + c0 + LANES].astype(F32)
                    gg = z_ref[rows, off_gs + c0:off_gs + c0 + LANES].astype(F32)
                    a_ref[rows, d_attn + c0:d_attn + c0 + LANES] = (
                        (u * mixed[g][:, j * blk:(j + 1) * blk]) * _silu(gg)).astype(BF16)

    assert n_blk == 2 and N_KV_HEADS == 2, "the emission schedule below is written for two blocks per tile"
    panel = {name: (off // MXU_WIDTH, width // MXU_WIDTH) for name, off, width in (
        ("q", 0, d_attn), ("kv", off_k, 2 * d_kv), ("ga", off_ga, d_attn),
        ("u", off_u, d_sgu), ("vs", off_vs, d_sgu), ("gs", off_gs, d_sgu))}
    assert all(c >= 1 for _, c in panel.values()) and d_in % MXU_WIDTH == 0

    n_out = d_model // MXU_WIDTH

    def step_body(first, between=lambda: None):
        kv_cache.clear()
        kvp_ref[...] = z_ref[t - blk:t, off_k:off_k + 2 * d_kv]

        def names(name):
            return list(range(panel[name][0], sum(panel[name])))

        gate_panels = [n for pair in zip(names("u"), names("gs")) for n in pair]
        rest = names("vs") + names("ga") + gate_panels

        def fill(count):
            for _ in range(min(count, len(rest))):
                in_panels(rest.pop(0), 1, between)

        if not first:
            out_panels(0, n_out // 2)
        modulated_norm()
        in_panels(*panel["q"], between)
        in_panels(*panel["kv"], between)
        fill(1)
        att = [attention(j, kv) for j in range(n_blk) for kv in range(N_KV_HEADS)]
        sgu = spatial_gating()
        for i, piece in enumerate(att):
            next(piece)
            if i == 0 and not first:
                out_panels(n_out // 2, n_out - n_out // 2)
            else:
                fill(4)
            next(piece)
            between()
            if i == 0 and not first:
                for j in range(n_blk):
                    finish(j)
            if i == 1:
                next(sgu)
            if i == 2:
                next(sgu)
        for piece in att:
            _drain(piece)
            between()
        next(sgu)
        while rest:
            fill(2)
            next(sgu, None)
        _drain(sgu)

    @pl.when(s_id == 0)
    def _():
        z_ref[t - blk:t, off_k:off_k + 2 * d_kv] = jnp.zeros((blk, 2 * d_kv), BF16)

        batch = c_ref.shape[0]
        c_act = _silu(c_ref[...])
        c_act = jnp.concatenate([c_act, jnp.zeros((mod_ref.shape[0] - batch, d_model), F32)], axis=0).astype(BF16)
        mod_ref[...] = jnp.broadcast_to(bada_ref[...], mod_ref.shape)

        def ada_piece(rows, cols, piece):
            mod_ref[:, cols] += jnp.dot(c_act[:, rows], piece.astype(BF16), preferred_element_type=F32)

        def win_piece(rows, cols, piece):
            win_ref[rows, cols] = piece.astype(BF16)

        def wout_piece(rows, cols, piece):
            wout_ref[rows, cols] = piece.astype(BF16)

        _drain(_stream_weight(wada_hbm, stage_ref, sem_ref, ada_piece, cols=(0, 2 * d_model)))
        _drain(_stream_weight(win_hbm, stage_ref, sem_ref, win_piece))

        def late_weights():
            yield from _stream_weight(wout_hbm, stage_ref, sem_ref, wout_piece)
            yield from _stream_weight(wada_hbm, stage_ref, sem_ref, ada_piece, cols=(2 * d_model, 3 * d_model))

        late = late_weights()
        step_body(True, between=lambda: next(late, None))
        _drain(late)

    @pl.when((s_id > 0) & (s_id < n_tiles))
    def _():
        step_body(False)

    @pl.when(s_id == n_tiles)
    def _():
        out_panels(0, n_out)
        for j in range(n_blk):
            finish(j)


def _layer(sinks, x2, c, w_ada, b_ada, norm_g, w_in, ln_g, ln_b, sgu_w, sgu_b, w_out, final_g, seq, d_attn, d_sgu):
    m, d = x2.shape
    t = TILE_TOKENS
    d_in = w_in.shape[1]
    d_mix = w_out.shape[0]
    d_kv = N_KV_HEADS * HEAD_DIM
    n_tiles = m // t
    mod_rows = 8
    assert c.shape[0] <= mod_rows

    def in_tile(s):
        return jnp.minimum(s, n_tiles - 1)

    def out_tile(s):
        return jnp.maximum(s - 1, 0)

    const2 = lambda s: (0, 0)
    const3 = lambda s: (0, 0, 0)
    return pl.pallas_call(
        functools.partial(_layer_kernel, d_attn=d_attn, d_sgu=d_sgu, steps_per_seq=seq // t, n_tiles=n_tiles),
        grid=(n_tiles + 1,),
        in_specs=[
            pl.BlockSpec(memory_space=pltpu.SMEM),
            pl.BlockSpec((t, d), lambda s: (in_tile(s), 0)),
            pl.BlockSpec((t, d), lambda s: (out_tile(s), 0)),
            pl.BlockSpec(c.shape, const2),
            pl.BlockSpec(memory_space=pl.ANY),
            pl.BlockSpec(b_ada.shape, const2),
            pl.BlockSpec((1, d), const2),
            pl.BlockSpec(memory_space=pl.ANY),
            pl.BlockSpec((1, d_sgu), const2),
            pl.BlockSpec((1, d_sgu), const2),
            pl.BlockSpec((SGU_GROUPS, WINDOW, WINDOW), const3),
            pl.BlockSpec((SGU_GROUPS, WINDOW), const2),
            pl.BlockSpec(memory_space=pl.ANY),
            pl.BlockSpec((1, d), const2),
        ],
        out_specs=pl.BlockSpec((t, d), lambda s: (out_tile(s), 0)),
        out_shape=jax.ShapeDtypeStruct((m, d), F32),
        scratch_shapes=[
            pltpu.VMEM((mod_rows, w_ada.shape[1]), F32),
            pltpu.VMEM((d, d_in), BF16),
            pltpu.VMEM((d_mix, d + LANES), BF16),
            pltpu.VMEM((STAGE_SLOTS, STAGE_ROWS, STAGE_COLS), F32),
            pltpu.SemaphoreType.DMA((STAGE_SLOTS,)),
            pltpu.VMEM((t, d), BF16),
            pltpu.VMEM((t, d_in), BF16),
            pltpu.VMEM((WINDOW, 2 * d_kv), BF16),
            pltpu.VMEM((t, d_mix), BF16),
            pltpu.VMEM((t, d), F32),
        ],
        compiler_params=pltpu.CompilerParams(
            dimension_semantics=("arbitrary",), vmem_limit_bytes=VMEM_LIMIT_BYTES),
        name="layer",
    )(sinks, x2, x2, c, w_ada, b_ada, norm_g, w_in, ln_g, ln_b, sgu_w, sgu_b, w_out, final_g)


def kernel(x, c, norm_g, w_ada, b_ada, w_in, attn_sinks, sgu_ln_g, sgu_ln_b, sgu_w, sgu_b, w_out, final_g):
    batch, seq, d = x.shape
    depth = norm_g.shape[0]
    n_q_heads = attn_sinks.shape[1]
    d_attn = n_q_heads * HEAD_DIM
    d_sgu = sgu_ln_g.shape[1]
    assert sgu_w.shape[1:] == (SGU_GROUPS, WINDOW, WINDOW) and d_sgu == SGU_GROUPS * LANES
    assert depth == 1, "stacked layers need the un-normalised residual stream between layers"

    x2 = x.reshape(batch * seq, d)
    out = _layer(attn_sinks[0], x2, c, w_ada[0], b_ada[0].reshape(1, -1), norm_g[0].reshape(1, d), w_in[0],
                 sgu_ln_g[0].reshape(1, d_sgu), sgu_ln_b[0].reshape(1, d_sgu), sgu_w[0], sgu_b[0],
                 w_out[0], final_g.reshape(1, d), seq, d_attn, d_sgu)
    return out.reshape(batch, seq, d)
```

```python
import functools
import math

import jax
import jax.numpy as jnp
from jax import lax
from jax.experimental import pallas as pl
from jax.experimental.pallas import tpu as pltpu

HEAD_DIM = 64
N_KV_HEADS = 2
WINDOW = 128
SGU_GROUPS = 8
EPS = 1e-6

F32 = jnp.float32
BF16 = jnp.bfloat16

LANES = 128
MXU_WIDTH = 256
VMEM_LIMIT_BYTES = 60 * 1024 * 1024
TILE_TOKENS = 256
STAGE_ROWS = 128
STAGE_COLS = 2688
STAGE_SLOTS = 4


def _silu(v):
    return v / (1.0 + jnp.exp(-v))


def _stream_weight(w_hbm, stage_ref, sem_ref, consume):
    k, n = w_hbm.shape
    col_parts = pl.cdiv(n, STAGE_COLS)
    width = n // col_parts
    assert k % STAGE_ROWS == 0 and n % col_parts == 0 and width % LANES == 0
    pieces = [(i, p) for i in range(k // STAGE_ROWS) for p in range(col_parts)]

    def copy(j):
        i, p = pieces[j]
        slot = j % STAGE_SLOTS
        return pltpu.make_async_copy(
            w_hbm.at[pl.ds(i * STAGE_ROWS, STAGE_ROWS), pl.ds(p * width, width)],
            stage_ref.at[slot, :, pl.ds(0, width)], sem_ref.at[slot])

    ahead = STAGE_SLOTS - 1
    for j in range(min(ahead, len(pieces))):
        copy(j).start()
    for j, (i, p) in enumerate(pieces):
        if j + ahead < len(pieces):
            copy(j + ahead).start()
        copy(j).wait()
        consume(slice(i * STAGE_ROWS, (i + 1) * STAGE_ROWS), slice(p * width, (p + 1) * width),
                stage_ref[j % STAGE_SLOTS, :, 0:width])


def _layer_kernel(sinks_ref, x_ref, xp_ref, c_ref, wada_hbm, bada_ref, ng_ref, win_hbm, lng_ref, lnb_ref,
                  sw_ref, sb_ref, wout_hbm, fg_ref, o_ref,
                  mod_ref, win_ref, wout_ref, stage_ref, sem_ref, h_ref, z_ref, kvp_ref, a_ref, y_ref,
                  *, d_attn, d_sgu, steps_per_seq, n_tiles):
    t, d_model = x_ref.shape
    s_id = pl.program_id(0)
    blk = WINDOW
    n_blk = t // blk
    d_kv = N_KV_HEADS * HEAD_DIM
    pairs_per_kv = d_attn // LANES // N_KV_HEADS
    rows4 = pairs_per_kv * blk

    off_k = d_attn
    off_v = off_k + d_kv
    off_ga = off_v + d_kv
    off_u = off_ga + d_attn
    off_vs = off_u + d_sgu
    off_gs = off_vs + d_sgu
    d_in = off_gs + d_sgu

    b_in = jnp.minimum(s_id, n_tiles - 1) // steps_per_seq
    b_out = jnp.maximum(s_id - 1, 0) // steps_per_seq
    first_in_seq = lax.rem(s_id, steps_per_seq) == 0
    pen0 = jnp.where(first_in_seq, -jnp.inf, 0.0).astype(F32)

    @pl.when(s_id == 0)
    def _():
        z_ref[t - blk:t, off_k:off_k + 2 * d_kv] = jnp.zeros((blk, 2 * d_kv), BF16)

        batch = c_ref.shape[0]
        c_act = _silu(c_ref[...])
        c_act = jnp.concatenate([c_act, jnp.zeros((mod_ref.shape[0] - batch, d_model), F32)], axis=0).astype(BF16)
        mod_ref[...] = jnp.broadcast_to(bada_ref[...], mod_ref.shape)

        def ada_piece(rows, cols, piece):
            mod_ref[:, cols] += jnp.dot(c_act[:, rows], piece.astype(BF16), preferred_element_type=F32)

        def win_piece(rows, cols, piece):
            win_ref[rows, cols] = piece.astype(BF16)

        def wout_piece(rows, cols, piece):
            wout_ref[rows, cols] = piece.astype(BF16)

        _stream_weight(wada_hbm, stage_ref, sem_ref, ada_piece)
        _stream_weight(win_hbm, stage_ref, sem_ref, win_piece)
        _stream_weight(wout_hbm, stage_ref, sem_ref, wout_piece)

    lane2 = lax.broadcasted_iota(jnp.int32, (2 * blk, LANES), 1)
    lo2 = lane2 < HEAD_DIM
    row = lax.broadcasted_iota(jnp.int32, (blk, blk), 0)
    col = lax.broadcasted_iota(jnp.int32, (blk, blk), 1)
    tri = col <= row
    key4 = lax.broadcasted_iota(jnp.int32, (blk, rows4), 0)
    qry4 = lax.broadcasted_iota(jnp.int32, (blk, rows4), 1) & (blk - 1)
    cur4 = key4 <= qry4
    sink_row4 = key4 == 0
    keycol = lax.broadcasted_iota(jnp.int32, (HEAD_DIM, 2 * blk), 1)
    scale = 1.0 / math.sqrt(HEAD_DIM)

    def modulated_norm():
        shift = mod_ref[pl.ds(b_in, 1), 0:d_model]
        gain = ng_ref[...] * (1.0 + mod_ref[pl.ds(b_in, 1), d_model:2 * d_model])
        for r in range(0, t, blk):
            x = x_ref[r:r + blk, :]
            ms = jnp.mean(x * x, axis=-1, keepdims=True)
            h_ref[r:r + blk, :] = ((x * lax.rsqrt(ms + EPS)) * gain + shift).astype(BF16)

    def _drain(generator):
        for _ in generator:
            pass

    def in_panels(first, count):
        for n in range(first, first + count):
            cs = slice(n * MXU_WIDTH, (n + 1) * MXU_WIDTH)
            z_ref[:, cs] = jnp.dot(h_ref[...], win_ref[:, cs], preferred_element_type=F32).astype(BF16)

    def out_panels(first, count):
        for n in range(first, first + count):
            cs = slice(n * MXU_WIDTH, (n + 1) * MXU_WIDTH)
            y_ref[:, cs] = jnp.dot(a_ref[...], wout_ref[:, cs], preferred_element_type=F32)

    def finish(j):
        rows = slice(j * blk, (j + 1) * blk)
        xn = xp_ref[rows, :] + mod_ref[pl.ds(b_out, 1), 2 * d_model:3 * d_model] * y_ref[rows, :]
        ms = jnp.mean(xn * xn, axis=-1, keepdims=True)
        o_ref[rows, :] = (xn * lax.rsqrt(ms + EPS)) * fg_ref[...]

    kv_cache = {}

    def block_kv(j):
        if j in kv_cache:
            return kv_cache[j]
        r0 = j * blk
        rows = slice(r0, r0 + blk)
        k_cur = z_ref[rows, off_k:off_k + d_kv]
        v_cur = z_ref[rows, off_v:off_v + d_kv]
        if j == 0:
            k_prev = kvp_ref[:, 0:d_kv]
            v_prev = kvp_ref[:, d_kv:2 * d_kv]
        else:
            k_prev = z_ref[r0 - blk:r0, off_k:off_k + d_kv]
            v_prev = z_ref[r0 - blk:r0, off_v:off_v + d_kv]
        kext = jnp.concatenate([k_prev, k_cur], axis=0).astype(F32) * scale
        kroll = pltpu.roll(kext, HEAD_DIM, 1)
        zero2 = jnp.zeros_like(kext)
        k_lo = [jnp.where(lo2, kext, zero2), jnp.where(lo2, kroll, zero2)]
        k_hi = [jnp.where(lo2, zero2, kroll), jnp.where(lo2, zero2, kext)]
        vext_t = jnp.concatenate([v_prev, v_cur], axis=0).astype(F32).T
        kv_cache[j] = (k_lo, k_hi, vext_t)
        return kv_cache[j]

    def attention(j, kv):
        rows = slice(j * blk, (j + 1) * blk)
        k_lo, k_hi, vext_t = block_kv(j)
        p0 = kv * pairs_per_kv
        kbd = jnp.concatenate([k_lo[kv], k_hi[kv]], axis=0).astype(BF16)
        q4 = jnp.concatenate(
            [z_ref[rows, (p0 + p) * LANES:(p0 + p + 1) * LANES] for p in range(pairs_per_kv)],
            axis=0)
        s_t = lax.dot_general(kbd, q4, (((1,), (1,)), ((), ())),
                              preferred_element_type=F32)
        yield
        v_t = vext_t[kv * HEAD_DIM:(kv + 1) * HEAD_DIM, :]
        v_t = jnp.where(keycol == 0, 0.0, v_t)
        v_aug = jnp.concatenate([v_t, jnp.ones((16, 2 * blk), F32)], axis=0).astype(BF16)

        pm = []
        for e in range(2):
            s_prev = s_t[2 * blk * e:2 * blk * e + blk, :]
            s_cur = s_t[2 * blk * e + blk:2 * blk * (e + 1), :]
            if j == 0:
                s_prev = s_prev + pen0
            sc = jnp.where(cur4, s_cur, s_prev)
            sink = jnp.concatenate(
                [jnp.full((1, blk), sinks_ref[2 * (p0 + p) + e], F32)
                 for p in range(pairs_per_kv)], axis=1)
            m = jnp.maximum(jnp.max(sc, axis=0, keepdims=True), sink)
            pr = jnp.exp(sc - m)
            p_sink = jnp.exp(sink - m)
            zero4 = jnp.zeros_like(pr)
            pm_prev = jnp.where(sink_row4, p_sink, jnp.where(cur4, zero4, pr))
            pm_cur = jnp.where(cur4, pr, zero4)
            pm.append(jnp.concatenate([pm_prev, pm_cur], axis=0).astype(BF16))

        att = []
        for p in range(pairs_per_kv):
            cs = slice(p * blk, (p + 1) * blk)
            pm_pair = jnp.concatenate([pm[0][:, cs], pm[1][:, cs]], axis=1)
            o_t = jnp.dot(v_aug, pm_pair, preferred_element_type=F32)
            inv = 1.0 / o_t[HEAD_DIM:HEAD_DIM + 8, :]
            inv = jnp.concatenate([inv] * (HEAD_DIM // 8), axis=0)
            att_t = o_t[0:HEAD_DIM, :] * inv
            att.append(jnp.concatenate([att_t[:, 0:blk], att_t[:, blk:2 * blk]], axis=0).T)
        yield
        for p in range(pairs_per_kv):
            c0 = (p0 + p) * LANES
            g = z_ref[rows, off_ga + c0:off_ga + c0 + LANES].astype(F32)
            a_ref[rows, c0:c0 + LANES] = (att[p] * _silu(g)).astype(BF16)

    def spatial_gating():
        vn = []
        for j in range(n_blk):
            vs = z_ref[j * blk:(j + 1) * blk, off_vs:off_vs + d_sgu].astype(F32)
            mu = jnp.mean(vs, axis=-1, keepdims=True)
            dv = vs - mu
            var = jnp.mean(dv * dv, axis=-1, keepdims=True)
            vn.append(((dv * lax.rsqrt(var + EPS)) * lng_ref[...] + lnb_ref[...]).astype(BF16))
        yield
        bias_t = sb_ref[...].T
        mixed = []
        for g in range(SGU_GROUPS):
            cs = slice(g * LANES, (g + 1) * LANES)
            w = jnp.where(tri, sw_ref[g], 0.0).astype(BF16)
            vn_g = jnp.concatenate([vn[j][:, cs] for j in range(n_blk)], axis=1)
            mixed.append(jnp.dot(w, vn_g, preferred_element_type=F32) + bias_t[:, g:g + 1])
        groups_per_panel = MXU_WIDTH // LANES
        for g0 in range(0, SGU_GROUPS, groups_per_panel):
            yield
            for g in range(g0, g0 + groups_per_panel):
                c0 = g * LANES
                for j in range(n_blk):
                    rows = slice(j * blk, (j + 1) * blk)
                    u = z_ref[rows, off_u + c0:off_u + c0 + LANES].astype(F32)
                    gg = z_ref[rows, off_gs + c0:off_gs + c0 + LANES].astype(F32)
                    a_ref[rows, d_attn + c0:d_attn + c0 + LANES] = (
                        (u * mixed[g][:, j * blk:(j + 1) * blk]) * _silu(gg)).astype(BF16)

    assert n_blk == 2 and N_KV_HEADS == 2, "the emission schedule below is written for two blocks per tile"
    panel = {name: (off // MXU_WIDTH, width // MXU_WIDTH) for name, off, width in (
        ("q", 0, d_attn), ("kv", off_k, 2 * d_kv), ("ga", off_ga, d_attn),
        ("u", off_u, d_sgu), ("vs", off_vs, d_sgu), ("gs", off_gs, d_sgu))}
    assert all(c >= 1 for _, c in panel.values()) and d_in % MXU_WIDTH == 0

    n_out = d_model // MXU_WIDTH

    def step_body(first):
        kv_cache.clear()
        kvp_ref[...] = z_ref[t - blk:t, off_k:off_k + 2 * d_kv]

        def names(name):
            return list(range(panel[name][0], sum(panel[name])))

        gate_panels = [n for pair in zip(names("u"), names("gs")) for n in pair]
        rest = names("vs") + names("ga") + gate_panels

        def fill(count):
            for _ in range(min(count, len(rest))):
                in_panels(rest.pop(0), 1)

        if not first:
            out_panels(0, n_out // 2)
        modulated_norm()
        in_panels(*panel["q"])
        in_panels(*panel["kv"])
        fill(1)
        att = [attention(j, kv) for j in range(n_blk) for kv in range(N_KV_HEADS)]
        sgu = spatial_gating()
        for i, piece in enumerate(att):
            next(piece)
            if i == 0 and not first:
                out_panels(n_out // 2, n_out - n_out // 2)
            else:
                fill(4)
            next(piece)
            if i == 0 and not first:
                for j in range(n_blk):
                    finish(j)
            if i == 1:
                next(sgu)
            if i == 2:
                next(sgu)
        for piece in att:
            _drain(piece)
        next(sgu)
        while rest:
            fill(2)
            next(sgu, None)
        _drain(sgu)

    @pl.when(s_id == 0)
    def _():
        step_body(True)

    @pl.when((s_id > 0) & (s_id < n_tiles))
    def _():
        step_body(False)

    @pl.when(s_id == n_tiles)
    def _():
        out_panels(0, n_out)
        for j in range(n_blk):
            finish(j)


def _layer(sinks, x2, c, w_ada, b_ada, norm_g, w_in, ln_g, ln_b, sgu_w, sgu_b, w_out, final_g, seq, d_attn, d_sgu):
    m, d = x2.shape
    t = TILE_TOKENS
    d_in = w_in.shape[1]
    d_mix = w_out.shape[0]
    d_kv = N_KV_HEADS * HEAD_DIM
    n_tiles = m // t
    mod_rows = 8
    assert c.shape[0] <= mod_rows

    def in_tile(s):
        return jnp.minimum(s, n_tiles - 1)

    def out_tile(s):
        return jnp.maximum(s - 1, 0)

    const2 = lambda s: (0, 0)
    const3 = lambda s: (0, 0, 0)
    return pl.pallas_call(
        functools.partial(_layer_kernel, d_attn=d_attn, d_sgu=d_sgu, steps_per_seq=seq // t, n_tiles=n_tiles),
        grid=(n_tiles + 1,),
        in_specs=[
            pl.BlockSpec(memory_space=pltpu.SMEM),
            pl.BlockSpec((t, d), lambda s: (in_tile(s), 0)),
            pl.BlockSpec((t, d), lambda s: (out_tile(s), 0)),
            pl.BlockSpec(c.shape, const2),
            pl.BlockSpec(memory_space=pl.ANY),
            pl.BlockSpec(b_ada.shape, const2),
            pl.BlockSpec((1, d), const2),
            pl.BlockSpec(memory_space=pl.ANY),
            pl.BlockSpec((1, d_sgu), const2),
            pl.BlockSpec((1, d_sgu), const2),
            pl.BlockSpec((SGU_GROUPS, WINDOW, WINDOW), const3),
            pl.BlockSpec((SGU_GROUPS, WINDOW), const2),
            pl.BlockSpec(memory_space=pl.ANY),
            pl.BlockSpec((1, d), const2),
        ],
        out_specs=pl.BlockSpec((t, d), lambda s: (out_tile(s), 0)),
        out_shape=jax.ShapeDtypeStruct((m, d), F32),
        scratch_shapes=[
            pltpu.VMEM((mod_rows, w_ada.shape[1]), F32),
            pltpu.VMEM((d, d_in), BF16),
            pltpu.VMEM((d_mix, d + LANES), BF16),
            pltpu.VMEM((STAGE_SLOTS, STAGE_ROWS, STAGE_COLS), F32),
            pltpu.SemaphoreType.DMA((STAGE_SLOTS,)),
            pltpu.VMEM((t, d), BF16),
            pltpu.VMEM((t, d_in), BF16),
            pltpu.VMEM((WINDOW, 2 * d_kv), BF16),
            pltpu.VMEM((t, d_mix), BF16),
            pltpu.VMEM((t, d), F32),
        ],
        compiler_params=pltpu.CompilerParams(
            dimension_semantics=("arbitrary",), vmem_limit_bytes=VMEM_LIMIT_BYTES),
        name="layer",
    )(sinks, x2, x2, c, w_ada, b_ada, norm_g, w_in, ln_g, ln_b, sgu_w, sgu_b, w_out, final_g)


def kernel(x, c, norm_g, w_ada, b_ada, w_in, attn_sinks, sgu_ln_g, sgu_ln_b, sgu_w, sgu_b, w_out, final_g):
    batch, seq, d = x.shape
    depth = norm_g.shape[0]
    n_q_heads = attn_sinks.shape[1]
    d_attn = n_q_heads * HEAD_DIM
    d_sgu = sgu_ln_g.shape[1]
    assert sgu_w.shape[1:] == (SGU_GROUPS, WINDOW, WINDOW) and d_sgu == SGU_GROUPS * LANES
    assert depth == 1, "stacked layers need the un-normalised residual stream between layers"

    x2 = x.reshape(batch * seq, d)
    out = _layer(attn_sinks[0], x2, c, w_ada[0], b_ada[0].reshape(1, -1), norm_g[0].reshape(1, d), w_in[0],
                 sgu_ln_g[0].reshape(1, d_sgu), sgu_ln_b[0].reshape(1, d_sgu), sgu_w[0], sgu_b[0],
                 w_out[0], final_g.reshape(1, d), seq, d_attn, d_sgu)
    return out.reshape(batch, seq, d)
```

```python
import functools
import math

import jax
import jax.numpy as jnp
from jax import lax
from jax.experimental import pallas as pl
from jax.experimental.pallas import tpu as pltpu

HEAD_DIM = 64
N_KV_HEADS = 2
WINDOW = 128
SGU_GROUPS = 8
EPS = 1e-6

F32 = jnp.float32
BF16 = jnp.bfloat16

LANES = 128
MXU_WIDTH = 256
VMEM_LIMIT_BYTES = 60 * 1024 * 1024
TILE_TOKENS = 256
STAGE_ROWS = 128
STAGE_COLS = 3072
STAGE_SLOTS = 4


def _silu(v):
    return v / (1.0 + jnp.exp(-v))


def _stream_weights(streams, stage_ref, sem_ref):
    pieces = []
    for w_hbm, consume in streams:
        k, n = w_hbm.shape
        col_parts = pl.cdiv(n, STAGE_COLS)
        width = n // col_parts
        assert k % STAGE_ROWS == 0 and n % col_parts == 0 and width % LANES == 0
        pieces += [(w_hbm, consume, i * STAGE_ROWS, p * width, width)
                   for i in range(k // STAGE_ROWS) for p in range(col_parts)]

    def copy(j):
        w_hbm, _, r0, c0, width = pieces[j]
        slot = j % STAGE_SLOTS
        return pltpu.make_async_copy(
            w_hbm.at[pl.ds(r0, STAGE_ROWS), pl.ds(c0, width)],
            stage_ref.at[slot, :, pl.ds(0, width)], sem_ref.at[slot])

    ahead = STAGE_SLOTS - 1
    for j in range(min(ahead, len(pieces))):
        copy(j).start()
    for j, (_, consume, r0, c0, width) in enumerate(pieces):
        if j + ahead < len(pieces):
            copy(j + ahead).start()
        copy(j).wait()
        consume(slice(r0, r0 + STAGE_ROWS), slice(c0, c0 + width), stage_ref[j % STAGE_SLOTS, :, 0:width])


def _layer_kernel(sinks_ref, x_ref, xp_ref, c_ref, wada_hbm, bada_ref, ng_ref, win_hbm, lng_ref, lnb_ref,
                  sw_ref, sb_ref, wout_hbm, fg_ref, o_ref,
                  mod_ref, win_ref, wout_ref, stage_ref, sem_ref, h_ref, z_ref, kvp_ref, a_ref, y_ref,
                  *, d_attn, d_sgu, steps_per_seq, n_tiles):
    t, d_model = x_ref.shape
    s_id = pl.program_id(0)
    blk = WINDOW
    n_blk = t // blk
    d_kv = N_KV_HEADS * HEAD_DIM
    pairs_per_kv = d_attn // LANES // N_KV_HEADS
    rows4 = pairs_per_kv * blk

    off_k = d_attn
    off_v = off_k + d_kv
    off_ga = off_v + d_kv
    off_u = off_ga + d_attn
    off_vs = off_u + d_sgu
    off_gs = off_vs + d_sgu
    d_in = off_gs + d_sgu

    b_in = jnp.minimum(s_id, n_tiles - 1) // steps_per_seq
    b_out = jnp.maximum(s_id - 1, 0) // steps_per_seq
    first_in_seq = lax.rem(s_id, steps_per_seq) == 0
    pen0 = jnp.where(first_in_seq, -jnp.inf, 0.0).astype(F32)

    @pl.when(s_id == 0)
    def _():
        a_ref[...] = jnp.zeros_like(a_ref)
        z_ref[t - blk:t, off_k:off_k + 2 * d_kv] = jnp.zeros((blk, 2 * d_kv), BF16)

        batch = c_ref.shape[0]
        c_act = _silu(c_ref[...])
        c_act = jnp.concatenate([c_act, jnp.zeros((mod_ref.shape[0] - batch, d_model), F32)], axis=0).astype(BF16)
        mod_ref[...] = jnp.broadcast_to(bada_ref[...], mod_ref.shape)

        def ada_piece(rows, cols, piece):
            mod_ref[:, cols] += jnp.dot(c_act[:, rows], piece.astype(BF16), preferred_element_type=F32)

        def win_piece(rows, cols, piece):
            win_ref[rows, cols] = piece.astype(BF16)

        def wout_piece(rows, cols, piece):
            wout_ref[rows, cols] = piece.astype(BF16)

        _stream_weights([(wada_hbm, ada_piece), (win_hbm, win_piece), (wout_hbm, wout_piece)], stage_ref, sem_ref)

    lane2 = lax.broadcasted_iota(jnp.int32, (2 * blk, LANES), 1)
    lo2 = lane2 < HEAD_DIM
    row = lax.broadcasted_iota(jnp.int32, (blk, blk), 0)
    col = lax.broadcasted_iota(jnp.int32, (blk, blk), 1)
    tri = col <= row
    key4 = lax.broadcasted_iota(jnp.int32, (blk, rows4), 0)
    qry4 = lax.broadcasted_iota(jnp.int32, (blk, rows4), 1) & (blk - 1)
    cur4 = key4 <= qry4
    sink_row4 = key4 == 0
    keycol = lax.broadcasted_iota(jnp.int32, (HEAD_DIM, 2 * blk), 1)
    scale = 1.0 / math.sqrt(HEAD_DIM)

    def modulated_norm():
        shift = mod_ref[pl.ds(b_in, 1), 0:d_model]
        gain = ng_ref[...] * (1.0 + mod_ref[pl.ds(b_in, 1), d_model:2 * d_model])
        for r in range(0, t, blk):
            x = x_ref[r:r + blk, :]
            ms = jnp.mean(x * x, axis=-1, keepdims=True)
            h_ref[r:r + blk, :] = ((x * lax.rsqrt(ms + EPS)) * gain + shift).astype(BF16)

    def in_panels(first, count):
        for n in range(first, first + count):
            cs = slice(n * MXU_WIDTH, (n + 1) * MXU_WIDTH)
            z_ref[:, cs] = jnp.dot(h_ref[...], win_ref[:, cs], preferred_element_type=F32).astype(BF16)

    def out_panels(first, count):
        for n in range(first, first + count):
            cs = slice(n * MXU_WIDTH, (n + 1) * MXU_WIDTH)
            y_ref[:, cs] = jnp.dot(a_ref[...], wout_ref[:, cs], preferred_element_type=F32)

    def finish(j):
        rows = slice(j * blk, (j + 1) * blk)
        xn = xp_ref[rows, :] + mod_ref[pl.ds(b_out, 1), 2 * d_model:3 * d_model] * y_ref[rows, :]
        ms = jnp.mean(xn * xn, axis=-1, keepdims=True)
        o_ref[rows, :] = (xn * lax.rsqrt(ms + EPS)) * fg_ref[...]

    kv_cache = {}

    def block_kv(j):
        if j in kv_cache:
            return kv_cache[j]
        r0 = j * blk
        rows = slice(r0, r0 + blk)
        k_cur = z_ref[rows, off_k:off_k + d_kv]
        v_cur = z_ref[rows, off_v:off_v + d_kv]
        if j == 0:
            k_prev = kvp_ref[:, 0:d_kv]
            v_prev = kvp_ref[:, d_kv:2 * d_kv]
        else:
            k_prev = z_ref[r0 - blk:r0, off_k:off_k + d_kv]
            v_prev = z_ref[r0 - blk:r0, off_v:off_v + d_kv]
        kext = jnp.concatenate([k_prev, k_cur], axis=0).astype(F32) * scale
        kroll = pltpu.roll(kext, HEAD_DIM, 1)
        zero2 = jnp.zeros_like(kext)
        k_lo = [jnp.where(lo2, kext, zero2), jnp.where(lo2, kroll, zero2)]
        k_hi = [jnp.where(lo2, zero2, kroll), jnp.where(lo2, zero2, kext)]
        vext_t = jnp.concatenate([v_prev, v_cur], axis=0).astype(F32).T
        kv_cache[j] = (k_lo, k_hi, vext_t)
        return kv_cache[j]

    def attention(j, kv):
        rows = slice(j * blk, (j + 1) * blk)
        k_lo, k_hi, vext_t = block_kv(j)
        p0 = kv * pairs_per_kv
        kbd = jnp.concatenate([k_lo[kv], k_hi[kv]], axis=0).astype(BF16)
        q4 = jnp.concatenate(
            [z_ref[rows, (p0 + p) * LANES:(p0 + p + 1) * LANES] for p in range(pairs_per_kv)],
            axis=0)
        s_t = lax.dot_general(kbd, q4, (((1,), (1,)), ((), ())),
                              preferred_element_type=F32)
        yield
        v_t = vext_t[kv * HEAD_DIM:(kv + 1) * HEAD_DIM, :]
        v_t = jnp.where(keycol == 0, 0.0, v_t)
        v_aug = jnp.concatenate([v_t, jnp.ones((16, 2 * blk), F32)], axis=0).astype(BF16)

        pm = []
        for e in range(2):
            s_prev = s_t[2 * blk * e:2 * blk * e + blk, :]
            s_cur = s_t[2 * blk * e + blk:2 * blk * (e + 1), :]
            if j == 0:
                s_prev = s_prev + pen0
            sc = jnp.where(cur4, s_cur, s_prev)
            sink = jnp.concatenate(
                [jnp.full((1, blk), sinks_ref[2 * (p0 + p) + e], F32)
                 for p in range(pairs_per_kv)], axis=1)
            m = jnp.maximum(jnp.max(sc, axis=0, keepdims=True), sink)
            pr = jnp.exp(sc - m)
            p_sink = jnp.exp(sink - m)
            zero4 = jnp.zeros_like(pr)
            pm_prev = jnp.where(sink_row4, p_sink, jnp.where(cur4, zero4, pr))
            pm_cur = jnp.where(cur4, pr, zero4)
            pm.append(jnp.concatenate([pm_prev, pm_cur], axis=0).astype(BF16))

        att = []
        for p in range(pairs_per_kv):
            cs = slice(p * blk, (p + 1) * blk)
            pm_pair = jnp.concatenate([pm[0][:, cs], pm[1][:, cs]], axis=1)
            o_t = jnp.dot(v_aug, pm_pair, preferred_element_type=F32)
            inv = 1.0 / o_t[HEAD_DIM:HEAD_DIM + 8, :]
            inv = jnp.concatenate([inv] * (HEAD_DIM // 8), axis=0)
            att_t = o_t[0:HEAD_DIM, :] * inv
            att.append(jnp.concatenate([att_t[:, 0:blk], att_t[:, blk:2 * blk]], axis=0).T)
        yield
        for p in range(pairs_per_kv):
            c0 = (p0 + p) * LANES
            g = z_ref[rows, off_ga + c0:off_ga + c0 + LANES].astype(F32)
            a_ref[rows, c0:c0 + LANES] = (att[p] * _silu(g)).astype(BF16)

    def spatial_gating():
        vn = []
        for j in range(n_blk):
            vs = z_ref[j * blk:(j + 1) * blk, off_vs:off_vs + d_sgu].astype(F32)
            mu = jnp.mean(vs, axis=-1, keepdims=True)
            dv = vs - mu
            var = jnp.mean(dv * dv, axis=-1, keepdims=True)
            vn.append(((dv * lax.rsqrt(var + EPS)) * lng_ref[...] + lnb_ref[...]).astype(BF16))
        yield
        bias_t = sb_ref[...].T
        mixed = []
        for g in range(SGU_GROUPS):
            cs = slice(g * LANES, (g + 1) * LANES)
            w = jnp.where(tri, sw_ref[g], 0.0).astype(BF16)
            vn_g = jnp.concatenate([vn[j][:, cs] for j in range(n_blk)], axis=1)
            mixed.append(jnp.dot(w, vn_g, preferred_element_type=F32) + bias_t[:, g:g + 1])
        groups_per_panel = MXU_WIDTH // LANES
        for g0 in range(0, SGU_GROUPS, groups_per_panel):
            yield
            for g in range(g0, g0 + groups_per_panel):
                c0 = g * LANES
                for j in range(n_blk):
                    rows = slice(j * blk, (j + 1) * blk)
                    u = z_ref[rows, off_u + c0:off_u + c0 + LANES].astype(F32)
                    gg = z_ref[rows, off_gs + c0:off_gs + c0 + LANES].astype(F32)
                    a_ref[rows, d_attn + c0:d_attn + c0 + LANES] = (
                        (u * mixed[g][:, j * blk:(j + 1) * blk]) * _silu(gg)).astype(BF16)

    assert n_blk == 2 and N_KV_HEADS == 2, "the emission schedule below is written for two blocks per tile"
    panel = {name: (off // MXU_WIDTH, width // MXU_WIDTH) for name, off, width in (
        ("q", 0, d_attn), ("kv", off_k, 2 * d_kv), ("ga", off_ga, d_attn),
        ("u", off_u, d_sgu), ("vs", off_vs, d_sgu), ("gs", off_gs, d_sgu))}
    assert all(c >= 1 for _, c in panel.values()) and d_in % MXU_WIDTH == 0

    n_out = d_model // MXU_WIDTH

    @pl.when(s_id < n_tiles)
    def _():
        kvp_ref[...] = z_ref[t - blk:t, off_k:off_k + 2 * d_kv]
        def names(name):
            return list(range(panel[name][0], sum(panel[name])))

        gate_panels = [n for pair in zip(names("u"), names("gs")) for n in pair]
        rest = names("vs") + names("ga") + gate_panels

        def fill(count):
            for _ in range(min(count, len(rest))):
                in_panels(rest.pop(0), 1)

        out_panels(0, n_out // 2)
        modulated_norm()
        in_panels(*panel["q"])
        in_panels(*panel["kv"])
        fill(1)
        att = [attention(j, kv) for j in range(n_blk) for kv in range(N_KV_HEADS)]
        sgu = spatial_gating()
        for i, piece in enumerate(att):
            next(piece)
            if i == 0:
                out_panels(n_out // 2, n_out - n_out // 2)
            else:
                fill(4)
            next(piece)
            if i == 0:
                for j in range(n_blk):
                    finish(j)
            if i == 1:
                next(sgu)
            if i == 2:
                next(sgu)
        for piece in att:
            for _ in piece:
                pass
        next(sgu)
        while rest:
            fill(2)
            next(sgu, None)
        for _ in sgu:
            pass

    @pl.when(s_id == n_tiles)
    def _():
        out_panels(0, n_out)
        for j in range(n_blk):
            finish(j)


def _layer(sinks, x2, c, w_ada, b_ada, norm_g, w_in, ln_g, ln_b, sgu_w, sgu_b, w_out, final_g, seq, d_attn, d_sgu):
    m, d = x2.shape
    t = TILE_TOKENS
    d_in = w_in.shape[1]
    d_mix = w_out.shape[0]
    d_kv = N_KV_HEADS * HEAD_DIM
    n_tiles = m // t
    mod_rows = 8
    assert c.shape[0] <= mod_rows

    def in_tile(s):
        return jnp.minimum(s, n_tiles - 1)

    def out_tile(s):
        return jnp.maximum(s - 1, 0)

    const2 = lambda s: (0, 0)
    const3 = lambda s: (0, 0, 0)
    return pl.pallas_call(
        functools.partial(_layer_kernel, d_attn=d_attn, d_sgu=d_sgu, steps_per_seq=seq // t, n_tiles=n_tiles),
        grid=(n_tiles + 1,),
        in_specs=[
            pl.BlockSpec(memory_space=pltpu.SMEM),
            pl.BlockSpec((t, d), lambda s: (in_tile(s), 0)),
            pl.BlockSpec((t, d), lambda s: (out_tile(s), 0)),
            pl.BlockSpec(c.shape, const2),
            pl.BlockSpec(memory_space=pl.ANY),
            pl.BlockSpec(b_ada.shape, const2),
            pl.BlockSpec((1, d), const2),
            pl.BlockSpec(memory_space=pl.ANY),
            pl.BlockSpec((1, d_sgu), const2),
            pl.BlockSpec((1, d_sgu), const2),
            pl.BlockSpec((SGU_GROUPS, WINDOW, WINDOW), const3),
            pl.BlockSpec((SGU_GROUPS, WINDOW), const2),
            pl.BlockSpec(memory_space=pl.ANY),
            pl.BlockSpec((1, d), const2),
        ],
        out_specs=pl.BlockSpec((t, d), lambda s: (out_tile(s), 0)),
        out_shape=jax.ShapeDtypeStruct((m, d), F32),
        scratch_shapes=[
            pltpu.VMEM((mod_rows, w_ada.shape[1]), F32),
            pltpu.VMEM((d, d_in), BF16),
            pltpu.VMEM((d_mix, d + LANES), BF16),
            pltpu.VMEM((STAGE_SLOTS, STAGE_ROWS, STAGE_COLS), F32),
            pltpu.SemaphoreType.DMA((STAGE_SLOTS,)),
            pltpu.VMEM((t, d), BF16),
            pltpu.VMEM((t, d_in), BF16),
            pltpu.VMEM((WINDOW, 2 * d_kv), BF16),
            pltpu.VMEM((t, d_mix), BF16),
            pltpu.VMEM((t, d), F32),
        ],
        compiler_params=pltpu.CompilerParams(
            dimension_semantics=("arbitrary",), vmem_limit_bytes=VMEM_LIMIT_BYTES),
        name="layer",
    )(sinks, x2, x2, c, w_ada, b_ada, norm_g, w_in, ln_g, ln_b, sgu_w, sgu_b, w_out, final_g)


def kernel(x, c, norm_g, w_ada, b_ada, w_in, attn_sinks, sgu_ln_g, sgu_ln_b, sgu_w, sgu_b, w_out, final_g):
    batch, seq, d = x.shape
    depth = norm_g.shape[0]
    n_q_heads = attn_sinks.shape[1]
    d_attn = n_q_heads * HEAD_DIM
    d_sgu = sgu_ln_g.shape[1]
    assert sgu_w.shape[1:] == (SGU_GROUPS, WINDOW, WINDOW) and d_sgu == SGU_GROUPS * LANES
    assert depth == 1, "stacked layers need the un-normalised residual stream between layers"

    x2 = x.reshape(batch * seq, d)
    out = _layer(attn_sinks[0], x2, c, w_ada[0], b_ada[0].reshape(1, -1), norm_g[0].reshape(1, d), w_in[0],
                 sgu_ln_g[0].reshape(1, d_sgu), sgu_ln_b[0].reshape(1, d_sgu), sgu_w[0], sgu_b[0],
                 w_out[0], final_g.reshape(1, d), seq, d_attn, d_sgu)
    return out.reshape(batch, seq, d)
```

```python
import functools
import math

import jax
import jax.numpy as jnp
from jax import lax
from jax.experimental import pallas as pl
from jax.experimental.pallas import tpu as pltpu

HEAD_DIM = 64
N_KV_HEADS = 2
WINDOW = 128
SGU_GROUPS = 8
EPS = 1e-6

F32 = jnp.float32
BF16 = jnp.bfloat16

LANES = 128
MXU_WIDTH = 256
VMEM_LIMIT_BYTES = 60 * 1024 * 1024
TILE_TOKENS = 256
STAGE_ROWS = 256
STAGE_COLS = 3072
STAGE_SLOTS = 2


def _silu(v):
    return v / (1.0 + jnp.exp(-v))


def _stream_weights(streams, stage_ref, sem_ref):
    pieces = []
    for w_hbm, consume in streams:
        k, n = w_hbm.shape
        col_parts = pl.cdiv(n, STAGE_COLS)
        width = n // col_parts
        assert k % STAGE_ROWS == 0 and n % col_parts == 0 and width % LANES == 0
        pieces += [(w_hbm, consume, i * STAGE_ROWS, p * width, width)
                   for i in range(k // STAGE_ROWS) for p in range(col_parts)]

    def copy(j):
        w_hbm, _, r0, c0, width = pieces[j]
        slot = j % STAGE_SLOTS
        return pltpu.make_async_copy(
            w_hbm.at[pl.ds(r0, STAGE_ROWS), pl.ds(c0, width)],
            stage_ref.at[slot, :, pl.ds(0, width)], sem_ref.at[slot])

    ahead = STAGE_SLOTS - 1
    for j in range(min(ahead, len(pieces))):
        copy(j).start()
    for j, (_, consume, r0, c0, width) in enumerate(pieces):
        if j + ahead < len(pieces):
            copy(j + ahead).start()
        copy(j).wait()
        consume(slice(r0, r0 + STAGE_ROWS), slice(c0, c0 + width), stage_ref[j % STAGE_SLOTS, :, 0:width])


def _layer_kernel(sinks_ref, x_ref, xp_ref, c_ref, wada_hbm, bada_ref, ng_ref, win_hbm, lng_ref, lnb_ref,
                  sw_ref, sb_ref, wout_hbm, fg_ref, o_ref,
                  mod_ref, win_ref, wout_ref, stage_ref, sem_ref, h_ref, z_ref, kvp_ref, a_ref, y_ref,
                  *, d_attn, d_sgu, steps_per_seq, n_tiles):
    t, d_model = x_ref.shape
    s_id = pl.program_id(0)
    blk = WINDOW
    n_blk = t // blk
    d_kv = N_KV_HEADS * HEAD_DIM
    pairs_per_kv = d_attn // LANES // N_KV_HEADS
    rows4 = pairs_per_kv * blk

    off_k = d_attn
    off_v = off_k + d_kv
    off_ga = off_v + d_kv
    off_u = off_ga + d_attn
    off_vs = off_u + d_sgu
    off_gs = off_vs + d_sgu
    d_in = off_gs + d_sgu

    b_in = jnp.minimum(s_id, n_tiles - 1) // steps_per_seq
    b_out = jnp.maximum(s_id - 1, 0) // steps_per_seq
    first_in_seq = lax.rem(s_id, steps_per_seq) == 0
    pen0 = jnp.where(first_in_seq, -jnp.inf, 0.0).astype(F32)

    @pl.when(s_id == 0)
    def _():
        a_ref[...] = jnp.zeros_like(a_ref)
        z_ref[t - blk:t, off_k:off_k + 2 * d_kv] = jnp.zeros((blk, 2 * d_kv), BF16)

        batch = c_ref.shape[0]
        c_act = _silu(c_ref[...])
        c_act = jnp.concatenate([c_act, jnp.zeros((mod_ref.shape[0] - batch, d_model), F32)], axis=0).astype(BF16)
        mod_ref[...] = jnp.broadcast_to(bada_ref[...], mod_ref.shape)

        def ada_piece(rows, cols, piece):
            mod_ref[:, cols] += jnp.dot(c_act[:, rows], piece.astype(BF16), preferred_element_type=F32)

        def win_piece(rows, cols, piece):
            win_ref[rows, cols] = piece.astype(BF16)

        def wout_piece(rows, cols, piece):
            wout_ref[rows, cols] = piece.astype(BF16)

        _stream_weights([(wada_hbm, ada_piece), (win_hbm, win_piece), (wout_hbm, wout_piece)], stage_ref, sem_ref)

    lane2 = lax.broadcasted_iota(jnp.int32, (2 * blk, LANES), 1)
    lo2 = lane2 < HEAD_DIM
    row = lax.broadcasted_iota(jnp.int32, (blk, blk), 0)
    col = lax.broadcasted_iota(jnp.int32, (blk, blk), 1)
    tri = col <= row
    key4 = lax.broadcasted_iota(jnp.int32, (blk, rows4), 0)
    qry4 = lax.broadcasted_iota(jnp.int32, (blk, rows4), 1) & (blk - 1)
    cur4 = key4 <= qry4
    sink_row4 = key4 == 0
    keycol = lax.broadcasted_iota(jnp.int32, (HEAD_DIM, 2 * blk), 1)
    scale = 1.0 / math.sqrt(HEAD_DIM)

    def modulated_norm():
        shift = mod_ref[pl.ds(b_in, 1), 0:d_model]
        gain = ng_ref[...] * (1.0 + mod_ref[pl.ds(b_in, 1), d_model:2 * d_model])
        for r in range(0, t, blk):
            x = x_ref[r:r + blk, :]
            ms = jnp.mean(x * x, axis=-1, keepdims=True)
            h_ref[r:r + blk, :] = ((x * lax.rsqrt(ms + EPS)) * gain + shift).astype(BF16)

    def in_panels(first, count):
        for n in range(first, first + count):
            cs = slice(n * MXU_WIDTH, (n + 1) * MXU_WIDTH)
            z_ref[:, cs] = jnp.dot(h_ref[...], win_ref[:, cs], preferred_element_type=F32).astype(BF16)

    def out_panels(first, count):
        for n in range(first, first + count):
            cs = slice(n * MXU_WIDTH, (n + 1) * MXU_WIDTH)
            y_ref[:, cs] = jnp.dot(a_ref[...], wout_ref[:, cs], preferred_element_type=F32)

    def finish(j):
        rows = slice(j * blk, (j + 1) * blk)
        xn = xp_ref[rows, :] + mod_ref[pl.ds(b_out, 1), 2 * d_model:3 * d_model] * y_ref[rows, :]
        ms = jnp.mean(xn * xn, axis=-1, keepdims=True)
        o_ref[rows, :] = (xn * lax.rsqrt(ms + EPS)) * fg_ref[...]

    kv_cache = {}

    def block_kv(j):
        if j in kv_cache:
            return kv_cache[j]
        r0 = j * blk
        rows = slice(r0, r0 + blk)
        k_cur = z_ref[rows, off_k:off_k + d_kv]
        v_cur = z_ref[rows, off_v:off_v + d_kv]
        if j == 0:
            k_prev = kvp_ref[:, 0:d_kv]
            v_prev = kvp_ref[:, d_kv:2 * d_kv]
        else:
            k_prev = z_ref[r0 - blk:r0, off_k:off_k + d_kv]
            v_prev = z_ref[r0 - blk:r0, off_v:off_v + d_kv]
        kext = jnp.concatenate([k_prev, k_cur], axis=0).astype(F32) * scale
        kroll = pltpu.roll(kext, HEAD_DIM, 1)
        zero2 = jnp.zeros_like(kext)
        k_lo = [jnp.where(lo2, kext, zero2), jnp.where(lo2, kroll, zero2)]
        k_hi = [jnp.where(lo2, zero2, kroll), jnp.where(lo2, zero2, kext)]
        vext_t = jnp.concatenate([v_prev, v_cur], axis=0).astype(F32).T
        kv_cache[j] = (k_lo, k_hi, vext_t)
        return kv_cache[j]

    def attention(j, kv):
        rows = slice(j * blk, (j + 1) * blk)
        k_lo, k_hi, vext_t = block_kv(j)
        p0 = kv * pairs_per_kv
        kbd = jnp.concatenate([k_lo[kv], k_hi[kv]], axis=0).astype(BF16)
        q4 = jnp.concatenate(
            [z_ref[rows, (p0 + p) * LANES:(p0 + p + 1) * LANES] for p in range(pairs_per_kv)],
            axis=0)
        s_t = lax.dot_general(kbd, q4, (((1,), (1,)), ((), ())),
                              preferred_element_type=F32)
        yield
        v_t = vext_t[kv * HEAD_DIM:(kv + 1) * HEAD_DIM, :]
        v_t = jnp.where(keycol == 0, 0.0, v_t)
        v_aug = jnp.concatenate([v_t, jnp.ones((16, 2 * blk), F32)], axis=0).astype(BF16)

        pm = []
        for e in range(2):
            s_prev = s_t[2 * blk * e:2 * blk * e + blk, :]
            s_cur = s_t[2 * blk * e + blk:2 * blk * (e + 1), :]
            if j == 0:
                s_prev = s_prev + pen0
            sc = jnp.where(cur4, s_cur, s_prev)
            sink = jnp.concatenate(
                [jnp.full((1, blk), sinks_ref[2 * (p0 + p) + e], F32)
                 for p in range(pairs_per_kv)], axis=1)
            m = jnp.maximum(jnp.max(sc, axis=0, keepdims=True), sink)
            pr = jnp.exp(sc - m)
            p_sink = jnp.exp(sink - m)
            zero4 = jnp.zeros_like(pr)
            pm_prev = jnp.where(sink_row4, p_sink, jnp.where(cur4, zero4, pr))
            pm_cur = jnp.where(cur4, pr, zero4)
            pm.append(jnp.concatenate([pm_prev, pm_cur], axis=0).astype(BF16))

        att = []
        for p in range(pairs_per_kv):
            cs = slice(p * blk, (p + 1) * blk)
            pm_pair = jnp.concatenate([pm[0][:, cs], pm[1][:, cs]], axis=1)
            o_t = jnp.dot(v_aug, pm_pair, preferred_element_type=F32)
            inv = 1.0 / o_t[HEAD_DIM:HEAD_DIM + 8, :]
            inv = jnp.concatenate([inv] * (HEAD_DIM // 8), axis=0)
            att_t = o_t[0:HEAD_DIM, :] * inv
            att.append(jnp.concatenate([att_t[:, 0:blk], att_t[:, blk:2 * blk]], axis=0).T)
        yield
        for p in range(pairs_per_kv):
            c0 = (p0 + p) * LANES
            g = z_ref[rows, off_ga + c0:off_ga + c0 + LANES].astype(F32)
            a_ref[rows, c0:c0 + LANES] = (att[p] * _silu(g)).astype(BF16)

    def spatial_gating():
        vn = []
        for j in range(n_blk):
            vs = z_ref[j * blk:(j + 1) * blk, off_vs:off_vs + d_sgu].astype(F32)
            mu = jnp.mean(vs, axis=-1, keepdims=True)
            dv = vs - mu
            var = jnp.mean(dv * dv, axis=-1, keepdims=True)
            vn.append(((dv * lax.rsqrt(var + EPS)) * lng_ref[...] + lnb_ref[...]).astype(BF16))
        yield
        bias_t = sb_ref[...].T
        mixed = []
        for g in range(SGU_GROUPS):
            cs = slice(g * LANES, (g + 1) * LANES)
            w = jnp.where(tri, sw_ref[g], 0.0).astype(BF16)
            vn_g = jnp.concatenate([vn[j][:, cs] for j in range(n_blk)], axis=1)
            mixed.append(jnp.dot(w, vn_g, preferred_element_type=F32) + bias_t[:, g:g + 1])
        groups_per_panel = MXU_WIDTH // LANES
        for g0 in range(0, SGU_GROUPS, groups_per_panel):
            yield
            for g in range(g0, g0 + groups_per_panel):
                c0 = g * LANES
                for j in range(n_blk):
                    rows = slice(j * blk, (j + 1) * blk)
                    u = z_ref[rows, off_u + c0:off_u + c0 + LANES].astype(F32)
                    gg = z_ref[rows, off_gs + c0:off_gs + c0 + LANES].astype(F32)
                    a_ref[rows, d_attn + c0:d_attn + c0 + LANES] = (
                        (u * mixed[g][:, j * blk:(j + 1) * blk]) * _silu(gg)).astype(BF16)

    assert n_blk == 2 and N_KV_HEADS == 2, "the emission schedule below is written for two blocks per tile"
    panel = {name: (off // MXU_WIDTH, width // MXU_WIDTH) for name, off, width in (
        ("q", 0, d_attn), ("kv", off_k, 2 * d_kv), ("ga", off_ga, d_attn),
        ("u", off_u, d_sgu), ("vs", off_vs, d_sgu), ("gs", off_gs, d_sgu))}
    assert all(c >= 1 for _, c in panel.values()) and d_in % MXU_WIDTH == 0

    n_out = d_model // MXU_WIDTH

    @pl.when(s_id < n_tiles)
    def _():
        kvp_ref[...] = z_ref[t - blk:t, off_k:off_k + 2 * d_kv]
        def names(name):
            return list(range(panel[name][0], sum(panel[name])))

        gate_panels = [n for pair in zip(names("u"), names("gs")) for n in pair]
        rest = names("vs") + names("ga") + gate_panels

        def fill(count):
            for _ in range(min(count, len(rest))):
                in_panels(rest.pop(0), 1)

        out_panels(0, n_out // 2)
        modulated_norm()
        in_panels(*panel["q"])
        in_panels(*panel["kv"])
        fill(1)
        att = [attention(j, kv) for j in range(n_blk) for kv in range(N_KV_HEADS)]
        sgu = spatial_gating()
        for i, piece in enumerate(att):
            next(piece)
            if i == 0:
                out_panels(n_out // 2, n_out - n_out // 2)
            else:
                fill(4)
            next(piece)
            if i == 0:
                for j in range(n_blk):
                    finish(j)
            if i == 1:
                next(sgu)
            if i == 2:
                next(sgu)
        for piece in att:
            for _ in piece:
                pass
        next(sgu)
        while rest:
            fill(2)
            next(sgu, None)
        for _ in sgu:
            pass

    @pl.when(s_id == n_tiles)
    def _():
        out_panels(0, n_out)
        for j in range(n_blk):
            finish(j)


def _layer(sinks, x2, c, w_ada, b_ada, norm_g, w_in, ln_g, ln_b, sgu_w, sgu_b, w_out, final_g, seq, d_attn, d_sgu):
    m, d = x2.shape
    t = TILE_TOKENS
    d_in = w_in.shape[1]
    d_mix = w_out.shape[0]
    d_kv = N_KV_HEADS * HEAD_DIM
    n_tiles = m // t
    mod_rows = 8
    assert c.shape[0] <= mod_rows

    def in_tile(s):
        return jnp.minimum(s, n_tiles - 1)

    def out_tile(s):
        return jnp.maximum(s - 1, 0)

    const2 = lambda s: (0, 0)
    const3 = lambda s: (0, 0, 0)
    return pl.pallas_call(
        functools.partial(_layer_kernel, d_attn=d_attn, d_sgu=d_sgu, steps_per_seq=seq // t, n_tiles=n_tiles),
        grid=(n_tiles + 1,),
        in_specs=[
            pl.BlockSpec(memory_space=pltpu.SMEM),
            pl.BlockSpec((t, d), lambda s: (in_tile(s), 0)),
            pl.BlockSpec((t, d), lambda s: (out_tile(s), 0)),
            pl.BlockSpec(c.shape, const2),
            pl.BlockSpec(memory_space=pl.ANY),
            pl.BlockSpec(b_ada.shape, const2),
            pl.BlockSpec((1, d), const2),
            pl.BlockSpec(memory_space=pl.ANY),
            pl.BlockSpec((1, d_sgu), const2),
            pl.BlockSpec((1, d_sgu), const2),
            pl.BlockSpec((SGU_GROUPS, WINDOW, WINDOW), const3),
            pl.BlockSpec((SGU_GROUPS, WINDOW), const2),
            pl.BlockSpec(memory_space=pl.ANY),
            pl.BlockSpec((1, d), const2),
        ],
        out_specs=pl.BlockSpec((t, d), lambda s: (out_tile(s), 0)),
        out_shape=jax.ShapeDtypeStruct((m, d), F32),
        scratch_shapes=[
            pltpu.VMEM((mod_rows, w_ada.shape[1]), F32),
            pltpu.VMEM((d, d_in), BF16),
            pltpu.VMEM((d_mix, d + LANES), BF16),
            pltpu.VMEM((STAGE_SLOTS, STAGE_ROWS, STAGE_COLS), F32),
            pltpu.SemaphoreType.DMA((STAGE_SLOTS,)),
            pltpu.VMEM((t, d), BF16),
            pltpu.VMEM((t, d_in), BF16),
            pltpu.VMEM((WINDOW, 2 * d_kv), BF16),
            pltpu.VMEM((t, d_mix), BF16),
            pltpu.VMEM((t, d), F32),
        ],
        compiler_params=pltpu.CompilerParams(
            dimension_semantics=("arbitrary",), vmem_limit_bytes=VMEM_LIMIT_BYTES),
        name="layer",
    )(sinks, x2, x2, c, w_ada, b_ada, norm_g, w_in, ln_g, ln_b, sgu_w, sgu_b, w_out, final_g)


def kernel(x, c, norm_g, w_ada, b_ada, w_in, attn_sinks, sgu_ln_g, sgu_ln_b, sgu_w, sgu_b, w_out, final_g):
    batch, seq, d = x.shape
    depth = norm_g.shape[0]
    n_q_heads = attn_sinks.shape[1]
    d_attn = n_q_heads * HEAD_DIM
    d_sgu = sgu_ln_g.shape[1]
    assert sgu_w.shape[1:] == (SGU_GROUPS, WINDOW, WINDOW) and d_sgu == SGU_GROUPS * LANES
    assert depth == 1, "stacked layers need the un-normalised residual stream between layers"

    x2 = x.reshape(batch * seq, d)
    out = _layer(attn_sinks[0], x2, c, w_ada[0], b_ada[0].reshape(1, -1), norm_g[0].reshape(1, d), w_in[0],
                 sgu_ln_g[0].reshape(1, d_sgu), sgu_ln_b[0].reshape(1, d_sgu), sgu_w[0], sgu_b[0],
                 w_out[0], final_g.reshape(1, d), seq, d_attn, d_sgu)
    return out.reshape(batch, seq, d)
```

```python
import functools
import math

import jax
import jax.numpy as jnp
from jax import lax
from jax.experimental import pallas as pl
from jax.experimental.pallas import tpu as pltpu

HEAD_DIM = 64
N_KV_HEADS = 2
WINDOW = 128
SGU_GROUPS = 8
EPS = 1e-6

F32 = jnp.float32
BF16 = jnp.bfloat16

LANES = 128
MXU_WIDTH = 256
VMEM_LIMIT_BYTES = 60 * 1024 * 1024
TILE_TOKENS = 256
STAGE_ROWS = 128
STAGE_COLS = 2048
STAGE_SLOTS = 6


def _silu(v):
    return v / (1.0 + jnp.exp(-v))


def _stream_weights(streams, stage_ref, sem_ref):
    pieces = []
    for w_hbm, consume in streams:
        k, n = w_hbm.shape
        col_parts = pl.cdiv(n, STAGE_COLS)
        width = n // col_parts
        assert k % STAGE_ROWS == 0 and n % col_parts == 0 and width % LANES == 0
        pieces += [(w_hbm, consume, i * STAGE_ROWS, p * width, width)
                   for i in range(k // STAGE_ROWS) for p in range(col_parts)]

    def copy(j):
        w_hbm, _, r0, c0, width = pieces[j]
        slot = j % STAGE_SLOTS
        return pltpu.make_async_copy(
            w_hbm.at[pl.ds(r0, STAGE_ROWS), pl.ds(c0, width)],
            stage_ref.at[slot, :, pl.ds(0, width)], sem_ref.at[slot])

    ahead = STAGE_SLOTS - 1
    for j in range(min(ahead, len(pieces))):
        copy(j).start()
    for j, (_, consume, r0, c0, width) in enumerate(pieces):
        if j + ahead < len(pieces):
            copy(j + ahead).start()
        copy(j).wait()
        consume(slice(r0, r0 + STAGE_ROWS), slice(c0, c0 + width), stage_ref[j % STAGE_SLOTS, :, 0:width])


def _layer_kernel(sinks_ref, x_ref, xp_ref, c_ref, wada_hbm, bada_ref, ng_ref, win_hbm, lng_ref, lnb_ref,
                  sw_ref, sb_ref, wout_hbm, fg_ref, o_ref,
                  mod_ref, win_ref, wout_ref, stage_ref, sem_ref, h_ref, z_ref, kvp_ref, a_ref, y_ref,
                  *, d_attn, d_sgu, steps_per_seq, n_tiles):
    t, d_model = x_ref.shape
    s_id = pl.program_id(0)
    blk = WINDOW
    n_blk = t // blk
    d_kv = N_KV_HEADS * HEAD_DIM
    pairs_per_kv = d_attn // LANES // N_KV_HEADS
    rows4 = pairs_per_kv * blk

    off_k = d_attn
    off_v = off_k + d_kv
    off_ga = off_v + d_kv
    off_u = off_ga + d_attn
    off_vs = off_u + d_sgu
    off_gs = off_vs + d_sgu
    d_in = off_gs + d_sgu

    b_in = jnp.minimum(s_id, n_tiles - 1) // steps_per_seq
    b_out = jnp.maximum(s_id - 1, 0) // steps_per_seq
    first_in_seq = lax.rem(s_id, steps_per_seq) == 0
    pen0 = jnp.where(first_in_seq, -jnp.inf, 0.0).astype(F32)

    @pl.when(s_id == 0)
    def _():
        a_ref[...] = jnp.zeros_like(a_ref)
        z_ref[t - blk:t, off_k:off_k + 2 * d_kv] = jnp.zeros((blk, 2 * d_kv), BF16)

        batch = c_ref.shape[0]
        c_act = _silu(c_ref[...])
        c_act = jnp.concatenate([c_act, jnp.zeros((mod_ref.shape[0] - batch, d_model), F32)], axis=0).astype(BF16)
        mod_ref[...] = jnp.broadcast_to(bada_ref[...], mod_ref.shape)

        def ada_piece(rows, cols, piece):
            mod_ref[:, cols] += jnp.dot(c_act[:, rows], piece.astype(BF16), preferred_element_type=F32)

        def win_piece(rows, cols, piece):
            win_ref[rows, cols] = piece.astype(BF16)

        def wout_piece(rows, cols, piece):
            wout_ref[rows, cols] = piece.astype(BF16)

        _stream_weights([(wada_hbm, ada_piece), (win_hbm, win_piece), (wout_hbm, wout_piece)], stage_ref, sem_ref)

    lane2 = lax.broadcasted_iota(jnp.int32, (2 * blk, LANES), 1)
    lo2 = lane2 < HEAD_DIM
    row = lax.broadcasted_iota(jnp.int32, (blk, blk), 0)
    col = lax.broadcasted_iota(jnp.int32, (blk, blk), 1)
    tri = col <= row
    key4 = lax.broadcasted_iota(jnp.int32, (blk, rows4), 0)
    qry4 = lax.broadcasted_iota(jnp.int32, (blk, rows4), 1) & (blk - 1)
    cur4 = key4 <= qry4
    sink_row4 = key4 == 0
    keycol = lax.broadcasted_iota(jnp.int32, (HEAD_DIM, 2 * blk), 1)
    scale = 1.0 / math.sqrt(HEAD_DIM)

    def modulated_norm():
        shift = mod_ref[pl.ds(b_in, 1), 0:d_model]
        gain = ng_ref[...] * (1.0 + mod_ref[pl.ds(b_in, 1), d_model:2 * d_model])
        for r in range(0, t, blk):
            x = x_ref[r:r + blk, :]
            ms = jnp.mean(x * x, axis=-1, keepdims=True)
            h_ref[r:r + blk, :] = ((x * lax.rsqrt(ms + EPS)) * gain + shift).astype(BF16)

    def in_panels(first, count):
        for n in range(first, first + count):
            cs = slice(n * MXU_WIDTH, (n + 1) * MXU_WIDTH)
            z_ref[:, cs] = jnp.dot(h_ref[...], win_ref[:, cs], preferred_element_type=F32).astype(BF16)

    def out_panels(first, count):
        for n in range(first, first + count):
            cs = slice(n * MXU_WIDTH, (n + 1) * MXU_WIDTH)
            y_ref[:, cs] = jnp.dot(a_ref[...], wout_ref[:, cs], preferred_element_type=F32)

    def finish(j):
        rows = slice(j * blk, (j + 1) * blk)
        xn = xp_ref[rows, :] + mod_ref[pl.ds(b_out, 1), 2 * d_model:3 * d_model] * y_ref[rows, :]
        ms = jnp.mean(xn * xn, axis=-1, keepdims=True)
        o_ref[rows, :] = (xn * lax.rsqrt(ms + EPS)) * fg_ref[...]

    kv_cache = {}

    def block_kv(j):
        if j in kv_cache:
            return kv_cache[j]
        r0 = j * blk
        rows = slice(r0, r0 + blk)
        k_cur = z_ref[rows, off_k:off_k + d_kv]
        v_cur = z_ref[rows, off_v:off_v + d_kv]
        if j == 0:
            k_prev = kvp_ref[:, 0:d_kv]
            v_prev = kvp_ref[:, d_kv:2 * d_kv]
        else:
            k_prev = z_ref[r0 - blk:r0, off_k:off_k + d_kv]
            v_prev = z_ref[r0 - blk:r0, off_v:off_v + d_kv]
        kext = jnp.concatenate([k_prev, k_cur], axis=0).astype(F32) * scale
        kroll = pltpu.roll(kext, HEAD_DIM, 1)
        zero2 = jnp.zeros_like(kext)
        k_lo = [jnp.where(lo2, kext, zero2), jnp.where(lo2, kroll, zero2)]
        k_hi = [jnp.where(lo2, zero2, kroll), jnp.where(lo2, zero2, kext)]
        vext_t = jnp.concatenate([v_prev, v_cur], axis=0).astype(F32).T
        kv_cache[j] = (k_lo, k_hi, vext_t)
        return kv_cache[j]

    def attention(j, kv):
        rows = slice(j * blk, (j + 1) * blk)
        k_lo, k_hi, vext_t = block_kv(j)
        p0 = kv * pairs_per_kv
        kbd = jnp.concatenate([k_lo[kv], k_hi[kv]], axis=0).astype(BF16)
        q4 = jnp.concatenate(
            [z_ref[rows, (p0 + p) * LANES:(p0 + p + 1) * LANES] for p in range(pairs_per_kv)],
            axis=0)
        s_t = lax.dot_general(kbd, q4, (((1,), (1,)), ((), ())),
                              preferred_element_type=F32)
        yield
        v_t = vext_t[kv * HEAD_DIM:(kv + 1) * HEAD_DIM, :]
        v_t = jnp.where(keycol == 0, 0.0, v_t)
        v_aug = jnp.concatenate([v_t, jnp.ones((16, 2 * blk), F32)], axis=0).astype(BF16)

        pm = []
        for e in range(2):
            s_prev = s_t[2 * blk * e:2 * blk * e + blk, :]
            s_cur = s_t[2 * blk * e + blk:2 * blk * (e + 1), :]
            if j == 0:
                s_prev = s_prev + pen0
            sc = jnp.where(cur4, s_cur, s_prev)
            sink = jnp.concatenate(
                [jnp.full((1, blk), sinks_ref[2 * (p0 + p) + e], F32)
                 for p in range(pairs_per_kv)], axis=1)
            m = jnp.maximum(jnp.max(sc, axis=0, keepdims=True), sink)
            pr = jnp.exp(sc - m)
            p_sink = jnp.exp(sink - m)
            zero4 = jnp.zeros_like(pr)
            pm_prev = jnp.where(sink_row4, p_sink, jnp.where(cur4, zero4, pr))
            pm_cur = jnp.where(cur4, pr, zero4)
            pm.append(jnp.concatenate([pm_prev, pm_cur], axis=0).astype(BF16))

        att = []
        for p in range(pairs_per_kv):
            cs = slice(p * blk, (p + 1) * blk)
            pm_pair = jnp.concatenate([pm[0][:, cs], pm[1][:, cs]], axis=1)
            o_t = jnp.dot(v_aug, pm_pair, preferred_element_type=F32)
            inv = 1.0 / o_t[HEAD_DIM:HEAD_DIM + 8, :]
            inv = jnp.concatenate([inv] * (HEAD_DIM // 8), axis=0)
            att_t = o_t[0:HEAD_DIM, :] * inv
            att.append(jnp.concatenate([att_t[:, 0:blk], att_t[:, blk:2 * blk]], axis=0).T)
        yield
        for p in range(pairs_per_kv):
            c0 = (p0 + p) * LANES
            g = z_ref[rows, off_ga + c0:off_ga + c0 + LANES].astype(F32)
            a_ref[rows, c0:c0 + LANES] = (att[p] * _silu(g)).astype(BF16)

    def spatial_gating():
        vn = []
        for j in range(n_blk):
            vs = z_ref[j * blk:(j + 1) * blk, off_vs:off_vs + d_sgu].astype(F32)
            mu = jnp.mean(vs, axis=-1, keepdims=True)
            dv = vs - mu
            var = jnp.mean(dv * dv, axis=-1, keepdims=True)
            vn.append(((dv * lax.rsqrt(var + EPS)) * lng_ref[...] + lnb_ref[...]).astype(BF16))
        yield
        bias_t = sb_ref[...].T
        mixed = []
        for g in range(SGU_GROUPS):
            cs = slice(g * LANES, (g + 1) * LANES)
            w = jnp.where(tri, sw_ref[g], 0.0).astype(BF16)
            vn_g = jnp.concatenate([vn[j][:, cs] for j in range(n_blk)], axis=1)
            mixed.append(jnp.dot(w, vn_g, preferred_element_type=F32) + bias_t[:, g:g + 1])
        groups_per_panel = MXU_WIDTH // LANES
        for g0 in range(0, SGU_GROUPS, groups_per_panel):
            yield
            for g in range(g0, g0 + groups_per_panel):
                c0 = g * LANES
                for j in range(n_blk):
                    rows = slice(j * blk, (j + 1) * blk)
                    u = z_ref[rows, off_u + c0:off_u + c0 + LANES].astype(F32)
                    gg = z_ref[rows, off_gs + c0:off_gs + c0 + LANES].astype(F32)
                    a_ref[rows, d_attn + c0:d_attn + c0 + LANES] = (
                        (u * mixed[g][:, j * blk:(j + 1) * blk]) * _silu(gg)).astype(BF16)

    assert n_blk == 2 and N_KV_HEADS == 2, "the emission schedule below is written for two blocks per tile"
    panel = {name: (off // MXU_WIDTH, width // MXU_WIDTH) for name, off, width in (
        ("q", 0, d_attn), ("kv", off_k, 2 * d_kv), ("ga", off_ga, d_attn),
        ("u", off_u, d_sgu), ("vs", off_vs, d_sgu), ("gs", off_gs, d_sgu))}
    assert all(c >= 1 for _, c in panel.values()) and d_in % MXU_WIDTH == 0

    n_out = d_model // MXU_WIDTH

    @pl.when(s_id < n_tiles)
    def _():
        kvp_ref[...] = z_ref[t - blk:t, off_k:off_k + 2 * d_kv]
        def names(name):
            return list(range(panel[name][0], sum(panel[name])))

        gate_panels = [n for pair in zip(names("u"), names("gs")) for n in pair]
        rest = names("vs") + names("ga") + gate_panels

        def fill(count):
            for _ in range(min(count, len(rest))):
                in_panels(rest.pop(0), 1)

        out_panels(0, n_out // 2)
        modulated_norm()
        in_panels(*panel["q"])
        in_panels(*panel["kv"])
        fill(1)
        att = [attention(j, kv) for j in range(n_blk) for kv in range(N_KV_HEADS)]
        sgu = spatial_gating()
        for i, piece in enumerate(att):
            next(piece)
            if i == 0:
                out_panels(n_out // 2, n_out - n_out // 2)
            else:
                fill(4)
            next(piece)
            if i == 0:
                for j in range(n_blk):
                    finish(j)
            if i == 1:
                next(sgu)
            if i == 2:
                next(sgu)
        for piece in att:
            for _ in piece:
                pass
        next(sgu)
        while rest:
            fill(2)
            next(sgu, None)
        for _ in sgu:
            pass

    @pl.when(s_id == n_tiles)
    def _():
        out_panels(0, n_out)
        for j in range(n_blk):
            finish(j)


def _layer(sinks, x2, c, w_ada, b_ada, norm_g, w_in, ln_g, ln_b, sgu_w, sgu_b, w_out, final_g, seq, d_attn, d_sgu):
    m, d = x2.shape
    t = TILE_TOKENS
    d_in = w_in.shape[1]
    d_mix = w_out.shape[0]
    d_kv = N_KV_HEADS * HEAD_DIM
    n_tiles = m // t
    mod_rows = 8
    assert c.shape[0] <= mod_rows

    def in_tile(s):
        return jnp.minimum(s, n_tiles - 1)

    def out_tile(s):
        return jnp.maximum(s - 1, 0)

    const2 = lambda s: (0, 0)
    const3 = lambda s: (0, 0, 0)
    return pl.pallas_call(
        functools.partial(_layer_kernel, d_attn=d_attn, d_sgu=d_sgu, steps_per_seq=seq // t, n_tiles=n_tiles),
        grid=(n_tiles + 1,),
        in_specs=[
            pl.BlockSpec(memory_space=pltpu.SMEM),
            pl.BlockSpec((t, d), lambda s: (in_tile(s), 0)),
            pl.BlockSpec((t, d), lambda s: (out_tile(s), 0)),
            pl.BlockSpec(c.shape, const2),
            pl.BlockSpec(memory_space=pl.ANY),
            pl.BlockSpec(b_ada.shape, const2),
            pl.BlockSpec((1, d), const2),
            pl.BlockSpec(memory_space=pl.ANY),
            pl.BlockSpec((1, d_sgu), const2),
            pl.BlockSpec((1, d_sgu), const2),
            pl.BlockSpec((SGU_GROUPS, WINDOW, WINDOW), const3),
            pl.BlockSpec((SGU_GROUPS, WINDOW), const2),
            pl.BlockSpec(memory_space=pl.ANY),
            pl.BlockSpec((1, d), const2),
        ],
        out_specs=pl.BlockSpec((t, d), lambda s: (out_tile(s), 0)),
        out_shape=jax.ShapeDtypeStruct((m, d), F32),
        scratch_shapes=[
            pltpu.VMEM((mod_rows, w_ada.shape[1]), F32),
            pltpu.VMEM((d, d_in), BF16),
            pltpu.VMEM((d_mix, d + LANES), BF16),
            pltpu.VMEM((STAGE_SLOTS, STAGE_ROWS, STAGE_COLS), F32),
            pltpu.SemaphoreType.DMA((STAGE_SLOTS,)),
            pltpu.VMEM((t, d), BF16),
            pltpu.VMEM((t, d_in), BF16),
            pltpu.VMEM((WINDOW, 2 * d_kv), BF16),
            pltpu.VMEM((t, d_mix), BF16),
            pltpu.VMEM((t, d), F32),
        ],
        compiler_params=pltpu.CompilerParams(
            dimension_semantics=("arbitrary",), vmem_limit_bytes=VMEM_LIMIT_BYTES),
        name="layer",
    )(sinks, x2, x2, c, w_ada, b_ada, norm_g, w_in, ln_g, ln_b, sgu_w, sgu_b, w_out, final_g)


def kernel(x, c, norm_g, w_ada, b_ada, w_in, attn_sinks, sgu_ln_g, sgu_ln_b, sgu_w, sgu_b, w_out, final_g):
    batch, seq, d = x.shape
    depth = norm_g.shape[0]
    n_q_heads = attn_sinks.shape[1]
    d_attn = n_q_heads * HEAD_DIM
    d_sgu = sgu_ln_g.shape[1]
    assert sgu_w.shape[1:] == (SGU_GROUPS, WINDOW, WINDOW) and d_sgu == SGU_GROUPS * LANES
    assert depth == 1, "stacked layers need the un-normalised residual stream between layers"

    x2 = x.reshape(batch * seq, d)
    out = _layer(attn_sinks[0], x2, c, w_ada[0], b_ada[0].reshape(1, -1), norm_g[0].reshape(1, d), w_in[0],
                 sgu_ln_g[0].reshape(1, d_sgu), sgu_ln_b[0].reshape(1, d_sgu), sgu_w[0], sgu_b[0],
                 w_out[0], final_g.reshape(1, d), seq, d_attn, d_sgu)
    return out.reshape(batch, seq, d)
```

```python
import functools
import math

import jax
import jax.numpy as jnp
from jax import lax
from jax.experimental import pallas as pl
from jax.experimental.pallas import tpu as pltpu

HEAD_DIM = 64
N_KV_HEADS = 2
WINDOW = 128
SGU_GROUPS = 8
EPS = 1e-6

F32 = jnp.float32
BF16 = jnp.bfloat16

LANES = 128
MXU_WIDTH = 256
VMEM_LIMIT_BYTES = 60 * 1024 * 1024
TILE_TOKENS = 256
STAGE_ROWS = 128
STAGE_COLS = 3072
STAGE_SLOTS = 4


def _silu(v):
    return v / (1.0 + jnp.exp(-v))


def _stream_weights(streams, stage_ref, sem_ref):
    pieces = []
    for w_hbm, consume in streams:
        k, n = w_hbm.shape
        col_parts = pl.cdiv(n, STAGE_COLS)
        width = n // col_parts
        assert k % STAGE_ROWS == 0 and n % col_parts == 0 and width % LANES == 0
        pieces += [(w_hbm, consume, i * STAGE_ROWS, p * width, width)
                   for i in range(k // STAGE_ROWS) for p in range(col_parts)]

    def copy(j):
        w_hbm, _, r0, c0, width = pieces[j]
        slot = j % STAGE_SLOTS
        return pltpu.make_async_copy(
            w_hbm.at[pl.ds(r0, STAGE_ROWS), pl.ds(c0, width)],
            stage_ref.at[slot, :, pl.ds(0, width)], sem_ref.at[slot])

    ahead = STAGE_SLOTS - 1
    for j in range(min(ahead, len(pieces))):
        copy(j).start()
    for j, (_, consume, r0, c0, width) in enumerate(pieces):
        if j + ahead < len(pieces):
            copy(j + ahead).start()
        copy(j).wait()
        consume(slice(r0, r0 + STAGE_ROWS), slice(c0, c0 + width), stage_ref[j % STAGE_SLOTS, :, 0:width])


def _layer_kernel(sinks_ref, x_ref, xp_ref, c_ref, wada_hbm, bada_ref, ng_ref, win_hbm, lng_ref, lnb_ref,
                  sw_ref, sb_ref, wout_hbm, fg_ref, o_ref,
                  mod_ref, win_ref, wout_ref, stage_ref, sem_ref, h_ref, z_ref, kvp_ref, a_ref, y_ref,
                  *, d_attn, d_sgu, steps_per_seq, n_tiles):
    t, d_model = x_ref.shape
    s_id = pl.program_id(0)
    blk = WINDOW
    n_blk = t // blk
    d_kv = N_KV_HEADS * HEAD_DIM
    pairs_per_kv = d_attn // LANES // N_KV_HEADS
    assert pairs_per_kv % 2 == 0

    off_k = d_attn
    off_v = off_k + d_kv
    off_ga = off_v + d_kv
    off_u = off_ga + d_attn
    off_vs = off_u + d_sgu
    off_gs = off_vs + d_sgu
    d_in = off_gs + d_sgu

    b_in = jnp.minimum(s_id, n_tiles - 1) // steps_per_seq
    b_out = jnp.maximum(s_id - 1, 0) // steps_per_seq
    first_in_seq = lax.rem(s_id, steps_per_seq) == 0
    pen0 = jnp.where(first_in_seq, -jnp.inf, 0.0).astype(F32)

    @pl.when(s_id == 0)
    def _():
        a_ref[...] = jnp.zeros_like(a_ref)
        z_ref[t - blk:t, off_k:off_k + 2 * d_kv] = jnp.zeros((blk, 2 * d_kv), BF16)

        batch = c_ref.shape[0]
        c_act = _silu(c_ref[...])
        c_act = jnp.concatenate([c_act, jnp.zeros((mod_ref.shape[0] - batch, d_model), F32)], axis=0).astype(BF16)
        mod_ref[...] = jnp.broadcast_to(bada_ref[...], mod_ref.shape)

        def ada_piece(rows, cols, piece):
            mod_ref[:, cols] += jnp.dot(c_act[:, rows], piece.astype(BF16), preferred_element_type=F32)

        def win_piece(rows, cols, piece):
            win_ref[rows, cols] = piece.astype(BF16)

        def wout_piece(rows, cols, piece):
            wout_ref[rows, cols] = piece.astype(BF16)

        _stream_weights([(wada_hbm, ada_piece), (win_hbm, win_piece), (wout_hbm, wout_piece)], stage_ref, sem_ref)

    lane2 = lax.broadcasted_iota(jnp.int32, (2 * blk, LANES), 1)
    lo2 = lane2 < HEAD_DIM
    row = lax.broadcasted_iota(jnp.int32, (blk, blk), 0)
    col = lax.broadcasted_iota(jnp.int32, (blk, blk), 1)
    tri = col <= row
    half = blk // 2
    lanes_h = pairs_per_kv * half
    key_h = lax.broadcasted_iota(jnp.int32, (half, lanes_h), 0)
    qry_h = lax.broadcasted_iota(jnp.int32, (half, lanes_h), 1) & (half - 1)
    cur_h = key_h <= qry_h
    first_row_h = key_h == 0
    lane_pair = lax.broadcasted_iota(jnp.int32, (1, lanes_h), 1) // half
    keycol = lax.broadcasted_iota(jnp.int32, (HEAD_DIM, 2 * blk), 1)
    scale = 1.0 / math.sqrt(HEAD_DIM)

    def modulated_norm():
        shift = mod_ref[pl.ds(b_in, 1), 0:d_model]
        gain = ng_ref[...] * (1.0 + mod_ref[pl.ds(b_in, 1), d_model:2 * d_model])
        for r in range(0, t, blk):
            x = x_ref[r:r + blk, :]
            ms = jnp.mean(x * x, axis=-1, keepdims=True)
            h_ref[r:r + blk, :] = ((x * lax.rsqrt(ms + EPS)) * gain + shift).astype(BF16)

    def in_panels(first, count):
        for n in range(first, first + count):
            cs = slice(n * MXU_WIDTH, (n + 1) * MXU_WIDTH)
            z_ref[:, cs] = jnp.dot(h_ref[...], win_ref[:, cs], preferred_element_type=F32).astype(BF16)

    def out_panels(first, count):
        for n in range(first, first + count):
            cs = slice(n * MXU_WIDTH, (n + 1) * MXU_WIDTH)
            y_ref[:, cs] = jnp.dot(a_ref[...], wout_ref[:, cs], preferred_element_type=F32)

    def finish(j):
        rows = slice(j * blk, (j + 1) * blk)
        xn = xp_ref[rows, :] + mod_ref[pl.ds(b_out, 1), 2 * d_model:3 * d_model] * y_ref[rows, :]
        ms = jnp.mean(xn * xn, axis=-1, keepdims=True)
        o_ref[rows, :] = (xn * lax.rsqrt(ms + EPS)) * fg_ref[...]

    kv_cache = {}

    def block_kv(j):
        if j in kv_cache:
            return kv_cache[j]
        r0 = j * blk
        rows = slice(r0, r0 + blk)
        k_cur = z_ref[rows, off_k:off_k + d_kv]
        v_cur = z_ref[rows, off_v:off_v + d_kv]
        if j == 0:
            k_prev = kvp_ref[:, 0:d_kv]
            v_prev = kvp_ref[:, d_kv:2 * d_kv]
        else:
            k_prev = z_ref[r0 - blk:r0, off_k:off_k + d_kv]
            v_prev = z_ref[r0 - blk:r0, off_v:off_v + d_kv]
        kext = jnp.concatenate([k_prev, k_cur], axis=0).astype(F32) * scale
        kroll = pltpu.roll(kext, HEAD_DIM, 1)
        zero2 = jnp.zeros_like(kext)
        k_lo = [jnp.where(lo2, kext, zero2), jnp.where(lo2, kroll, zero2)]
        k_hi = [jnp.where(lo2, zero2, kroll), jnp.where(lo2, zero2, kext)]
        vext_t = jnp.concatenate([v_prev, v_cur], axis=0).astype(F32).T
        kv_cache[j] = (k_lo, k_hi, vext_t)
        return kv_cache[j]

    def attention(j, kv):
        r0 = j * blk
        k_lo, k_hi, vext_t = block_kv(j)
        p0 = kv * pairs_per_kv
        key_rows = [slice(0, blk + half), slice(half, 2 * blk)]
        n_keys = blk + half
        s_h = []
        for h in range(2):
            kbd = jnp.concatenate([k_lo[kv][key_rows[h]], k_hi[kv][key_rows[h]]], axis=0).astype(BF16)
            q_h = jnp.concatenate(
                [z_ref[r0 + h * half:r0 + (h + 1) * half, (p0 + p) * LANES:(p0 + p + 1) * LANES]
                 for p in range(pairs_per_kv)], axis=0)
            s_h.append(lax.dot_general(kbd, q_h, (((1,), (1,)), ((), ())), preferred_element_type=F32))
        yield
        v_t = vext_t[kv * HEAD_DIM:(kv + 1) * HEAD_DIM, :]
        ones = jnp.ones((16, 2 * blk), F32)
        att = {}
        for h in range(2):
            v_aug = jnp.concatenate([jnp.where(keycol == h * half, 0.0, v_t), ones], axis=0).astype(BF16)
            pm = []
            for e in range(2):
                s = s_h[h][e * n_keys:(e + 1) * n_keys, :]
                s_a, s_b, s_c = s[0:half], s[half:blk], s[blk:n_keys]
                if h == 0:
                    if j == 0:
                        s_a, s_b = s_a + pen0, s_b + pen0
                    sc = jnp.concatenate([jnp.where(cur_h, s_c, s_a), s_b], axis=0)
                else:
                    if j == 0:
                        s_a = s_a + pen0
                    sc = jnp.concatenate([s_b, jnp.where(cur_h, s_c, s_a)], axis=0)
                sink = jnp.zeros((1, lanes_h), F32)
                for p in range(pairs_per_kv):
                    sink = jnp.where(lane_pair == p, sinks_ref[2 * (p0 + p) + e], sink)
                m = jnp.maximum(jnp.max(sc, axis=0, keepdims=True), sink)
                pr = jnp.exp(sc - m)
                p_sink = jnp.exp(sink - m)
                top, bot = pr[0:half], pr[half:blk]
                zero = jnp.zeros_like(top)
                if h == 0:
                    prev_keys = [jnp.where(first_row_h, p_sink, jnp.where(cur_h, zero, top)), bot]
                    cur_keys = [jnp.where(cur_h, top, zero), zero]
                else:
                    prev_keys = [zero, jnp.where(first_row_h, p_sink, jnp.where(cur_h, zero, bot))]
                    cur_keys = [top, jnp.where(cur_h, bot, zero)]
                pm.append(jnp.concatenate(prev_keys + cur_keys, axis=0).astype(BF16))

            for tl in range(pairs_per_kv // 2):
                cs = slice(tl * LANES, (tl + 1) * LANES)
                w = jnp.concatenate([pm[0][:, cs], pm[1][:, cs]], axis=1)
                o_t = jnp.dot(v_aug, w, preferred_element_type=F32)
                inv = 1.0 / o_t[HEAD_DIM:HEAD_DIM + 8, :]
                inv = jnp.concatenate([inv] * (HEAD_DIM // 8), axis=0)
                att_t = o_t[0:HEAD_DIM, :] * inv
                att[(h, tl)] = jnp.concatenate([att_t[:, 0:LANES], att_t[:, LANES:2 * LANES]], axis=0).T
        yield
        for (h, tl), a in att.items():
            rows = slice(r0 + h * half, r0 + (h + 1) * half)
            for u in range(2):
                c0 = (p0 + 2 * tl + u) * LANES
                g = z_ref[rows, off_ga + c0:off_ga + c0 + LANES].astype(F32)
                a_ref[rows, c0:c0 + LANES] = (a[u * half:(u + 1) * half, :] * _silu(g)).astype(BF16)

    def spatial_gating():
        vn = []
        for j in range(n_blk):
            vs = z_ref[j * blk:(j + 1) * blk, off_vs:off_vs + d_sgu].astype(F32)
            mu = jnp.mean(vs, axis=-1, keepdims=True)
            dv = vs - mu
            var = jnp.mean(dv * dv, axis=-1, keepdims=True)
            vn.append(((dv * lax.rsqrt(var + EPS)) * lng_ref[...] + lnb_ref[...]).astype(BF16))
        yield
        bias_t = sb_ref[...].T
        mixed = []
        for g in range(SGU_GROUPS):
            cs = slice(g * LANES, (g + 1) * LANES)
            w = jnp.where(tri, sw_ref[g], 0.0).astype(BF16)
            vn_g = jnp.concatenate([vn[j][:, cs] for j in range(n_blk)], axis=1)
            mixed.append(jnp.dot(w, vn_g, preferred_element_type=F32) + bias_t[:, g:g + 1])
        groups_per_panel = MXU_WIDTH // LANES
        for g0 in range(0, SGU_GROUPS, groups_per_panel):
            yield
            for g in range(g0, g0 + groups_per_panel):
                c0 = g * LANES
                for j in range(n_blk):
                    rows = slice(j * blk, (j + 1) * blk)
                    u = z_ref[rows, off_u + c0:off_u + c0 + LANES].astype(F32)
                    gg = z_ref[rows, off_gs + c0:off_gs + c0 + LANES].astype(F32)
                    a_ref[rows, d_attn + c0:d_attn + c0 + LANES] = (
                        (u * mixed[g][:, j * blk:(j + 1) * blk]) * _silu(gg)).astype(BF16)

    assert n_blk == 2 and N_KV_HEADS == 2, "the emission schedule below is written for two blocks per tile"
    panel = {name: (off // MXU_WIDTH, width // MXU_WIDTH) for name, off, width in (
        ("q", 0, d_attn), ("kv", off_k, 2 * d_kv), ("ga", off_ga, d_attn),
        ("u", off_u, d_sgu), ("vs", off_vs, d_sgu), ("gs", off_gs, d_sgu))}
    assert all(c >= 1 for _, c in panel.values()) and d_in % MXU_WIDTH == 0

    n_out = d_model // MXU_WIDTH

    @pl.when(s_id < n_tiles)
    def _():
        kvp_ref[...] = z_ref[t - blk:t, off_k:off_k + 2 * d_kv]
        def names(name):
            return list(range(panel[name][0], sum(panel[name])))

        gate_panels = [n for pair in zip(names("u"), names("gs")) for n in pair]
        rest = names("vs") + names("ga") + gate_panels

        def fill(count):
            for _ in range(min(count, len(rest))):
                in_panels(rest.pop(0), 1)

        out_panels(0, n_out // 2)
        modulated_norm()
        in_panels(*panel["q"])
        in_panels(*panel["kv"])
        fill(1)
        att = [attention(j, kv) for j in range(n_blk) for kv in range(N_KV_HEADS)]
        sgu = spatial_gating()
        for i, piece in enumerate(att):
            next(piece)
            if i == 0:
                out_panels(n_out // 2, n_out - n_out // 2)
            else:
                fill(4)
            next(piece)
            if i == 0:
                for j in range(n_blk):
                    finish(j)
            if i == 1:
                next(sgu)
            if i == 2:
                next(sgu)
        for piece in att:
            for _ in piece:
                pass
        next(sgu)
        while rest:
            fill(2)
            next(sgu, None)
        for _ in sgu:
            pass

    @pl.when(s_id == n_tiles)
    def _():
        out_panels(0, n_out)
        for j in range(n_blk):
            finish(j)


def _layer(sinks, x2, c, w_ada, b_ada, norm_g, w_in, ln_g, ln_b, sgu_w, sgu_b, w_out, final_g, seq, d_attn, d_sgu):
    m, d = x2.shape
    t = TILE_TOKENS
    d_in = w_in.shape[1]
    d_mix = w_out.shape[0]
    d_kv = N_KV_HEADS * HEAD_DIM
    n_tiles = m // t
    mod_rows = 8
    assert c.shape[0] <= mod_rows

    def in_tile(s):
        return jnp.minimum(s, n_tiles - 1)

    def out_tile(s):
        return jnp.maximum(s - 1, 0)

    const2 = lambda s: (0, 0)
    const3 = lambda s: (0, 0, 0)
    return pl.pallas_call(
        functools.partial(_layer_kernel, d_attn=d_attn, d_sgu=d_sgu, steps_per_seq=seq // t, n_tiles=n_tiles),
        grid=(n_tiles + 1,),
        in_specs=[
            pl.BlockSpec(memory_space=pltpu.SMEM),
            pl.BlockSpec((t, d), lambda s: (in_tile(s), 0)),
            pl.BlockSpec((t, d), lambda s: (out_tile(s), 0)),
            pl.BlockSpec(c.shape, const2),
            pl.BlockSpec(memory_space=pl.ANY),
            pl.BlockSpec(b_ada.shape, const2),
            pl.BlockSpec((1, d), const2),
            pl.BlockSpec(memory_space=pl.ANY),
            pl.BlockSpec((1, d_sgu), const2),
            pl.BlockSpec((1, d_sgu), const2),
            pl.BlockSpec((SGU_GROUPS, WINDOW, WINDOW), const3),
            pl.BlockSpec((SGU_GROUPS, WINDOW), const2),
            pl.BlockSpec(memory_space=pl.ANY),
            pl.BlockSpec((1, d), const2),
        ],
        out_specs=pl.BlockSpec((t, d), lambda s: (out_tile(s), 0)),
        out_shape=jax.ShapeDtypeStruct((m, d), F32),
        scratch_shapes=[
            pltpu.VMEM((mod_rows, w_ada.shape[1]), F32),
            pltpu.VMEM((d, d_in), BF16),
            pltpu.VMEM((d_mix, d + LANES), BF16),
            pltpu.VMEM((STAGE_SLOTS, STAGE_ROWS, STAGE_COLS), F32),
            pltpu.SemaphoreType.DMA((STAGE_SLOTS,)),
            pltpu.VMEM((t, d), BF16),
            pltpu.VMEM((t, d_in), BF16),
            pltpu.VMEM((WINDOW, 2 * d_kv), BF16),
            pltpu.VMEM((t, d_mix), BF16),
            pltpu.VMEM((t, d), F32),
        ],
        compiler_params=pltpu.CompilerParams(
            dimension_semantics=("arbitrary",), vmem_limit_bytes=VMEM_LIMIT_BYTES),
        name="layer",
    )(sinks, x2, x2, c, w_ada, b_ada, norm_g, w_in, ln_g, ln_b, sgu_w, sgu_b, w_out, final_g)


def kernel(x, c, norm_g, w_ada, b_ada, w_in, attn_sinks, sgu_ln_g, sgu_ln_b, sgu_w, sgu_b, w_out, final_g):
    batch, seq, d = x.shape
    depth = norm_g.shape[0]
    n_q_heads = attn_sinks.shape[1]
    d_attn = n_q_heads * HEAD_DIM
    d_sgu = sgu_ln_g.shape[1]
    assert sgu_w.shape[1:] == (SGU_GROUPS, WINDOW, WINDOW) and d_sgu == SGU_GROUPS * LANES
    assert depth == 1, "stacked layers need the un-normalised residual stream between layers"

    x2 = x.reshape(batch * seq, d)
    out = _layer(attn_sinks[0], x2, c, w_ada[0], b_ada[0].reshape(1, -1), norm_g[0].reshape(1, d), w_in[0],
                 sgu_ln_g[0].reshape(1, d_sgu), sgu_ln_b[0].reshape(1, d_sgu), sgu_w[0], sgu_b[0],
                 w_out[0], final_g.reshape(1, d), seq, d_attn, d_sgu)
    return out.reshape(batch, seq, d)
```

```python
import functools
import math

import jax
import jax.numpy as jnp
from jax import lax
from jax.experimental import pallas as pl
from jax.experimental.pallas import tpu as pltpu

HEAD_DIM = 64
N_KV_HEADS = 2
WINDOW = 128
SGU_GROUPS = 8
EPS = 1e-6

F32 = jnp.float32
BF16 = jnp.bfloat16

LANES = 128
MXU_WIDTH = 256
VMEM_LIMIT_BYTES = 60 * 1024 * 1024
TILE_TOKENS = 256
STAGE_ROWS = 128
STAGE_COLS = 3072
STAGE_SLOTS = 4


def _silu(v):
    return v / (1.0 + jnp.exp(-v))


def _stream_weights(streams, stage_ref, sem_ref):
    pieces = []
    for w_hbm, consume in streams:
        k, n = w_hbm.shape
        col_parts = pl.cdiv(n, STAGE_COLS)
        width = n // col_parts
        assert k % STAGE_ROWS == 0 and n % col_parts == 0 and width % LANES == 0
        pieces += [(w_hbm, consume, i * STAGE_ROWS, p * width, width)
                   for i in range(k // STAGE_ROWS) for p in range(col_parts)]

    def copy(j):
        w_hbm, _, r0, c0, width = pieces[j]
        slot = j % STAGE_SLOTS
        return pltpu.make_async_copy(
            w_hbm.at[pl.ds(r0, STAGE_ROWS), pl.ds(c0, width)],
            stage_ref.at[slot, :, pl.ds(0, width)], sem_ref.at[slot])

    ahead = STAGE_SLOTS - 1
    for j in range(min(ahead, len(pieces))):
        copy(j).start()
    for j, (_, consume, r0, c0, width) in enumerate(pieces):
        if j + ahead < len(pieces):
            copy(j + ahead).start()
        copy(j).wait()
        consume(slice(r0, r0 + STAGE_ROWS), slice(c0, c0 + width), stage_ref[j % STAGE_SLOTS, :, 0:width])


def _layer_kernel(sinks_ref, x_ref, xp_ref, c_ref, wada_hbm, bada_ref, ng_ref, win_hbm, lng_ref, lnb_ref,
                  sw_ref, sb_ref, wout_hbm, fg_ref, o_ref,
                  mod_ref, win_ref, wout_ref, stage_ref, sem_ref, h_ref, z_ref, kvp_ref, a_ref, y_ref,
                  *, d_attn, d_sgu, steps_per_seq, n_tiles):
    t, d_model = x_ref.shape
    s_id = pl.program_id(0)
    blk = WINDOW
    n_blk = t // blk
    d_kv = N_KV_HEADS * HEAD_DIM
    pairs_per_kv = d_attn // LANES // N_KV_HEADS
    assert pairs_per_kv % 2 == 0

    off_k = d_attn
    off_v = off_k + d_kv
    off_ga = off_v + d_kv
    off_u = off_ga + d_attn
    off_vs = off_u + d_sgu
    off_gs = off_vs + d_sgu
    d_in = off_gs + d_sgu

    b_in = jnp.minimum(s_id, n_tiles - 1) // steps_per_seq
    b_out = jnp.maximum(s_id - 1, 0) // steps_per_seq
    first_in_seq = lax.rem(s_id, steps_per_seq) == 0
    pen0 = jnp.where(first_in_seq, -jnp.inf, 0.0).astype(F32)

    @pl.when(s_id == 0)
    def _():
        a_ref[...] = jnp.zeros_like(a_ref)
        z_ref[t - blk:t, off_k:off_k + 2 * d_kv] = jnp.zeros((blk, 2 * d_kv), BF16)

        batch = c_ref.shape[0]
        c_act = _silu(c_ref[...])
        c_act = jnp.concatenate([c_act, jnp.zeros((mod_ref.shape[0] - batch, d_model), F32)], axis=0).astype(BF16)
        mod_ref[...] = jnp.broadcast_to(bada_ref[...], mod_ref.shape)

        def ada_piece(rows, cols, piece):
            mod_ref[:, cols] += jnp.dot(c_act[:, rows], piece.astype(BF16), preferred_element_type=F32)

        def win_piece(rows, cols, piece):
            win_ref[rows, cols] = piece.astype(BF16)

        def wout_piece(rows, cols, piece):
            wout_ref[rows, cols] = piece.astype(BF16)

        _stream_weights([(wada_hbm, ada_piece), (win_hbm, win_piece), (wout_hbm, wout_piece)], stage_ref, sem_ref)

    lane2 = lax.broadcasted_iota(jnp.int32, (2 * blk, LANES), 1)
    lo2 = lane2 < HEAD_DIM
    row = lax.broadcasted_iota(jnp.int32, (blk, blk), 0)
    col = lax.broadcasted_iota(jnp.int32, (blk, blk), 1)
    tri = col <= row
    half = blk // 2
    lanes_h = pairs_per_kv * half
    key_h = lax.broadcasted_iota(jnp.int32, (half, lanes_h), 0)
    qry_h = lax.broadcasted_iota(jnp.int32, (half, lanes_h), 1) & (half - 1)
    cur_h = key_h <= qry_h
    first_row_h = key_h == 0
    lane_pair = lax.broadcasted_iota(jnp.int32, (1, lanes_h), 1) // half
    key_rows = [slice(0, blk + half), slice(half, 2 * blk)]
    n_keys = blk + half
    keycol = lax.broadcasted_iota(jnp.int32, (HEAD_DIM, n_keys), 1)
    scale = 1.0 / math.sqrt(HEAD_DIM)

    def modulated_norm():
        shift = mod_ref[pl.ds(b_in, 1), 0:d_model]
        gain = ng_ref[...] * (1.0 + mod_ref[pl.ds(b_in, 1), d_model:2 * d_model])
        for r in range(0, t, blk):
            x = x_ref[r:r + blk, :]
            ms = jnp.mean(x * x, axis=-1, keepdims=True)
            h_ref[r:r + blk, :] = ((x * lax.rsqrt(ms + EPS)) * gain + shift).astype(BF16)

    def in_panels(first, count):
        for n in range(first, first + count):
            cs = slice(n * MXU_WIDTH, (n + 1) * MXU_WIDTH)
            z_ref[:, cs] = jnp.dot(h_ref[...], win_ref[:, cs], preferred_element_type=F32).astype(BF16)

    def out_panels(first, count):
        for n in range(first, first + count):
            cs = slice(n * MXU_WIDTH, (n + 1) * MXU_WIDTH)
            y_ref[:, cs] = jnp.dot(a_ref[...], wout_ref[:, cs], preferred_element_type=F32)

    def finish(j):
        rows = slice(j * blk, (j + 1) * blk)
        xn = xp_ref[rows, :] + mod_ref[pl.ds(b_out, 1), 2 * d_model:3 * d_model] * y_ref[rows, :]
        ms = jnp.mean(xn * xn, axis=-1, keepdims=True)
        o_ref[rows, :] = (xn * lax.rsqrt(ms + EPS)) * fg_ref[...]

    kv_cache = {}

    def block_kv(j):
        if j in kv_cache:
            return kv_cache[j]
        r0 = j * blk
        rows = slice(r0, r0 + blk)
        k_cur = z_ref[rows, off_k:off_k + d_kv]
        v_cur = z_ref[rows, off_v:off_v + d_kv]
        if j == 0:
            k_prev = kvp_ref[:, 0:d_kv]
            v_prev = kvp_ref[:, d_kv:2 * d_kv]
        else:
            k_prev = z_ref[r0 - blk:r0, off_k:off_k + d_kv]
            v_prev = z_ref[r0 - blk:r0, off_v:off_v + d_kv]
        kext = jnp.concatenate([k_prev, k_cur], axis=0).astype(F32) * scale
        kroll = pltpu.roll(kext, HEAD_DIM, 1)
        zero2 = jnp.zeros_like(kext)
        k_lo = [jnp.where(lo2, kext, zero2), jnp.where(lo2, kroll, zero2)]
        k_hi = [jnp.where(lo2, zero2, kroll), jnp.where(lo2, zero2, kext)]
        vext = jnp.concatenate([v_prev, v_cur], axis=0).astype(F32)
        vext_t = [vext[rows_h, :].T for rows_h in key_rows]
        kv_cache[j] = (k_lo, k_hi, vext_t)
        return kv_cache[j]

    def attention(j, kv):
        r0 = j * blk
        k_lo, k_hi, vext_t = block_kv(j)
        p0 = kv * pairs_per_kv
        s_h = []
        for h in range(2):
            kbd = jnp.concatenate([k_lo[kv][key_rows[h]], k_hi[kv][key_rows[h]]], axis=0).astype(BF16)
            q_h = jnp.concatenate(
                [z_ref[r0 + h * half:r0 + (h + 1) * half, (p0 + p) * LANES:(p0 + p + 1) * LANES]
                 for p in range(pairs_per_kv)], axis=0)
            s_h.append(lax.dot_general(kbd, q_h, (((1,), (1,)), ((), ())), preferred_element_type=F32))
        yield
        ones = jnp.ones((16, n_keys), F32)
        att = {}
        for h in range(2):
            v_t = vext_t[h][kv * HEAD_DIM:(kv + 1) * HEAD_DIM, :]
            v_aug = jnp.concatenate([jnp.where(keycol == 0, 0.0, v_t), ones], axis=0).astype(BF16)
            pm = []
            for e in range(2):
                s = s_h[h][e * n_keys:(e + 1) * n_keys, :]
                s_a, s_b, s_c = s[0:half], s[half:blk], s[blk:n_keys]
                if h == 0:
                    if j == 0:
                        s_a, s_b = s_a + pen0, s_b + pen0
                    sc = jnp.concatenate([jnp.where(cur_h, s_c, s_a), s_b], axis=0)
                else:
                    if j == 0:
                        s_a = s_a + pen0
                    sc = jnp.concatenate([s_b, jnp.where(cur_h, s_c, s_a)], axis=0)
                sink = jnp.zeros((1, lanes_h), F32)
                for p in range(pairs_per_kv):
                    sink = jnp.where(lane_pair == p, sinks_ref[2 * (p0 + p) + e], sink)
                m = jnp.maximum(jnp.max(sc, axis=0, keepdims=True), sink)
                pr = jnp.exp(sc - m)
                p_sink = jnp.exp(sink - m)
                top, bot = pr[0:half], pr[half:blk]
                zero = jnp.zeros_like(top)
                if h == 0:
                    visible = [jnp.where(first_row_h, p_sink, jnp.where(cur_h, zero, top)), bot,
                               jnp.where(cur_h, top, zero)]
                else:
                    visible = [jnp.where(first_row_h, p_sink, jnp.where(cur_h, zero, bot)), top,
                               jnp.where(cur_h, bot, zero)]
                pm.append(jnp.concatenate(visible, axis=0).astype(BF16))

            for tl in range(pairs_per_kv // 2):
                cs = slice(tl * LANES, (tl + 1) * LANES)
                w = jnp.concatenate([pm[0][:, cs], pm[1][:, cs]], axis=1)
                o_t = jnp.dot(v_aug, w, preferred_element_type=F32)
                inv = 1.0 / o_t[HEAD_DIM:HEAD_DIM + 8, :]
                inv = jnp.concatenate([inv] * (HEAD_DIM // 8), axis=0)
                att_t = o_t[0:HEAD_DIM, :] * inv
                att[(h, tl)] = jnp.concatenate([att_t[:, 0:LANES], att_t[:, LANES:2 * LANES]], axis=0).T
                yield "pv"
        yield
        for (h, tl), a in att.items():
            rows = slice(r0 + h * half, r0 + (h + 1) * half)
            for u in range(2):
                c0 = (p0 + 2 * tl + u) * LANES
                g = z_ref[rows, off_ga + c0:off_ga + c0 + LANES].astype(F32)
                a_ref[rows, c0:c0 + LANES] = (a[u * half:(u + 1) * half, :] * _silu(g)).astype(BF16)

    def spatial_gating():
        vn = []
        for j in range(n_blk):
            vs = z_ref[j * blk:(j + 1) * blk, off_vs:off_vs + d_sgu].astype(F32)
            mu = jnp.mean(vs, axis=-1, keepdims=True)
            dv = vs - mu
            var = jnp.mean(dv * dv, axis=-1, keepdims=True)
            vn.append(((dv * lax.rsqrt(var + EPS)) * lng_ref[...] + lnb_ref[...]).astype(BF16))
        yield
        bias_t = sb_ref[...].T
        mixed = []
        for g in range(SGU_GROUPS):
            cs = slice(g * LANES, (g + 1) * LANES)
            w = jnp.where(tri, sw_ref[g], 0.0).astype(BF16)
            vn_g = jnp.concatenate([vn[j][:, cs] for j in range(n_blk)], axis=1)
            mixed.append(jnp.dot(w, vn_g, preferred_element_type=F32) + bias_t[:, g:g + 1])
        groups_per_panel = MXU_WIDTH // LANES
        for g0 in range(0, SGU_GROUPS, groups_per_panel):
            yield
            for g in range(g0, g0 + groups_per_panel):
                c0 = g * LANES
                for j in range(n_blk):
                    rows = slice(j * blk, (j + 1) * blk)
                    u = z_ref[rows, off_u + c0:off_u + c0 + LANES].astype(F32)
                    gg = z_ref[rows, off_gs + c0:off_gs + c0 + LANES].astype(F32)
                    a_ref[rows, d_attn + c0:d_attn + c0 + LANES] = (
                        (u * mixed[g][:, j * blk:(j + 1) * blk]) * _silu(gg)).astype(BF16)

    assert n_blk == 2 and N_KV_HEADS == 2, "the emission schedule below is written for two blocks per tile"
    panel = {name: (off // MXU_WIDTH, width // MXU_WIDTH) for name, off, width in (
        ("q", 0, d_attn), ("kv", off_k, 2 * d_kv), ("ga", off_ga, d_attn),
        ("u", off_u, d_sgu), ("vs", off_vs, d_sgu), ("gs", off_gs, d_sgu))}
    assert all(c >= 1 for _, c in panel.values()) and d_in % MXU_WIDTH == 0

    n_out = d_model // MXU_WIDTH

    @pl.when(s_id < n_tiles)
    def _():
        kvp_ref[...] = z_ref[t - blk:t, off_k:off_k + 2 * d_kv]
        def names(name):
            return list(range(panel[name][0], sum(panel[name])))

        gate_panels = [n for pair in zip(names("u"), names("gs")) for n in pair]
        rest = names("vs") + names("ga") + gate_panels

        def fill(count):
            for _ in range(min(count, len(rest))):
                in_panels(rest.pop(0), 1)

        out_panels(0, n_out // 2)
        modulated_norm()
        in_panels(*panel["q"])
        in_panels(*panel["kv"])
        fill(1)
        att = [attention(j, kv) for j in range(n_blk) for kv in range(N_KV_HEADS)]
        sgu = spatial_gating()
        for i, piece in enumerate(att):
            next(piece)
            if i == 0:
                fillers = [functools.partial(out_panels, n, 1) for n in range(n_out // 2, n_out)]
            else:
                fillers = [functools.partial(fill, 1)] * 4
            for filler in fillers[:2]:
                filler()
            later = fillers[2:]
            while next(piece) == "pv":
                if later:
                    later.pop(0)()
            if i == 0:
                for j in range(n_blk):
                    finish(j)
            if i == 1:
                next(sgu)
            if i == 2:
                next(sgu)
        for piece in att:
            for _ in piece:
                pass
        next(sgu)
        while rest:
            fill(2)
            next(sgu, None)
        for _ in sgu:
            pass

    @pl.when(s_id == n_tiles)
    def _():
        out_panels(0, n_out)
        for j in range(n_blk):
            finish(j)


def _layer(sinks, x2, c, w_ada, b_ada, norm_g, w_in, ln_g, ln_b, sgu_w, sgu_b, w_out, final_g, seq, d_attn, d_sgu):
    m, d = x2.shape
    t = TILE_TOKENS
    d_in = w_in.shape[1]
    d_mix = w_out.shape[0]
    d_kv = N_KV_HEADS * HEAD_DIM
    n_tiles = m // t
    mod_rows = 8
    assert c.shape[0] <= mod_rows

    def in_tile(s):
        return jnp.minimum(s, n_tiles - 1)

    def out_tile(s):
        return jnp.maximum(s - 1, 0)

    const2 = lambda s: (0, 0)
    const3 = lambda s: (0, 0, 0)
    return pl.pallas_call(
        functools.partial(_layer_kernel, d_attn=d_attn, d_sgu=d_sgu, steps_per_seq=seq // t, n_tiles=n_tiles),
        grid=(n_tiles + 1,),
        in_specs=[
            pl.BlockSpec(memory_space=pltpu.SMEM),
            pl.BlockSpec((t, d), lambda s: (in_tile(s), 0)),
            pl.BlockSpec((t, d), lambda s: (out_tile(s), 0)),
            pl.BlockSpec(c.shape, const2),
            pl.BlockSpec(memory_space=pl.ANY),
            pl.BlockSpec(b_ada.shape, const2),
            pl.BlockSpec((1, d), const2),
            pl.BlockSpec(memory_space=pl.ANY),
            pl.BlockSpec((1, d_sgu), const2),
            pl.BlockSpec((1, d_sgu), const2),
            pl.BlockSpec((SGU_GROUPS, WINDOW, WINDOW), const3),
            pl.BlockSpec((SGU_GROUPS, WINDOW), const2),
            pl.BlockSpec(memory_space=pl.ANY),
            pl.BlockSpec((1, d), const2),
        ],
        out_specs=pl.BlockSpec((t, d), lambda s: (out_tile(s), 0)),
        out_shape=jax.ShapeDtypeStruct((m, d), F32),
        scratch_shapes=[
            pltpu.VMEM((mod_rows, w_ada.shape[1]), F32),
            pltpu.VMEM((d, d_in), BF16),
            pltpu.VMEM((d_mix, d + LANES), BF16),
            pltpu.VMEM((STAGE_SLOTS, STAGE_ROWS, STAGE_COLS), F32),
            pltpu.SemaphoreType.DMA((STAGE_SLOTS,)),
            pltpu.VMEM((t, d), BF16),
            pltpu.VMEM((t, d_in), BF16),
            pltpu.VMEM((WINDOW, 2 * d_kv), BF16),
            pltpu.VMEM((t, d_mix), BF16),
            pltpu.VMEM((t, d), F32),
        ],
        compiler_params=pltpu.CompilerParams(
            dimension_semantics=("arbitrary",), vmem_limit_bytes=VMEM_LIMIT_BYTES),
        name="layer",
    )(sinks, x2, x2, c, w_ada, b_ada, norm_g, w_in, ln_g, ln_b, sgu_w, sgu_b, w_out, final_g)


def kernel(x, c, norm_g, w_ada, b_ada, w_in, attn_sinks, sgu_ln_g, sgu_ln_b, sgu_w, sgu_b, w_out, final_g):
    batch, seq, d = x.shape
    depth = norm_g.shape[0]
    n_q_heads = attn_sinks.shape[1]
    d_attn = n_q_heads * HEAD_DIM
    d_sgu = sgu_ln_g.shape[1]
    assert sgu_w.shape[1:] == (SGU_GROUPS, WINDOW, WINDOW) and d_sgu == SGU_GROUPS * LANES
    assert depth == 1, "stacked layers need the un-normalised residual stream between layers"

    x2 = x.reshape(batch * seq, d)
    out = _layer(attn_sinks[0], x2, c, w_ada[0], b_ada[0].reshape(1, -1), norm_g[0].reshape(1, d), w_in[0],
                 sgu_ln_g[0].reshape(1, d_sgu), sgu_ln_b[0].reshape(1, d_sgu), sgu_w[0], sgu_b[0],
                 w_out[0], final_g.reshape(1, d), seq, d_attn, d_sgu)
    return out.reshape(batch, seq, d)
```

```python
import functools
import math

import jax
import jax.numpy as jnp
from jax import lax
from jax.experimental import pallas as pl
from jax.experimental.pallas import tpu as pltpu

HEAD_DIM = 64
N_KV_HEADS = 2
WINDOW = 128
SGU_GROUPS = 8
EPS = 1e-6

F32 = jnp.float32
BF16 = jnp.bfloat16

LANES = 128
MXU_WIDTH = 256
VMEM_LIMIT_BYTES = 60 * 1024 * 1024
TILE_TOKENS = 256
STAGE_ROWS = 128
STAGE_COLS = 3072
STAGE_SLOTS = 4


def _silu(v):
    return v / (1.0 + jnp.exp(-v))


def _stream_weights(streams, stage_ref, sem_ref):
    pieces = []
    for w_hbm, consume in streams:
        k, n = w_hbm.shape
        col_parts = pl.cdiv(n, STAGE_COLS)
        width = n // col_parts
        assert k % STAGE_ROWS == 0 and n % col_parts == 0 and width % LANES == 0
        pieces += [(w_hbm, consume, i * STAGE_ROWS, p * width, width)
                   for i in range(k // STAGE_ROWS) for p in range(col_parts)]

    def copy(j):
        w_hbm, _, r0, c0, width = pieces[j]
        slot = j % STAGE_SLOTS
        return pltpu.make_async_copy(
            w_hbm.at[pl.ds(r0, STAGE_ROWS), pl.ds(c0, width)],
            stage_ref.at[slot, :, pl.ds(0, width)], sem_ref.at[slot])

    ahead = STAGE_SLOTS - 1
    for j in range(min(ahead, len(pieces))):
        copy(j).start()
    for j, (_, consume, r0, c0, width) in enumerate(pieces):
        if j + ahead < len(pieces):
            copy(j + ahead).start()
        copy(j).wait()
        consume(slice(r0, r0 + STAGE_ROWS), slice(c0, c0 + width), stage_ref[j % STAGE_SLOTS, :, 0:width])


def _layer_kernel(sinks_ref, x_ref, xp_ref, c_ref, wada_hbm, bada_ref, ng_ref, win_hbm, lng_ref, lnb_ref,
                  sw_ref, sb_ref, wout_hbm, fg_ref, o_ref,
                  mod_ref, win_ref, wout_ref, stage_ref, sem_ref, h_ref, z_ref, kvp_ref, a_ref, y_ref,
                  *, d_attn, d_sgu, steps_per_seq, n_tiles):
    t, d_model = x_ref.shape
    s_id = pl.program_id(0)
    blk = WINDOW
    n_blk = t // blk
    d_kv = N_KV_HEADS * HEAD_DIM
    pairs_per_kv = d_attn // LANES // N_KV_HEADS
    assert pairs_per_kv % 2 == 0

    off_k = d_attn
    off_v = off_k + d_kv
    off_ga = off_v + d_kv
    off_u = off_ga + d_attn
    off_vs = off_u + d_sgu
    off_gs = off_vs + d_sgu
    d_in = off_gs + d_sgu

    b_in = jnp.minimum(s_id, n_tiles - 1) // steps_per_seq
    b_out = jnp.maximum(s_id - 1, 0) // steps_per_seq
    first_in_seq = lax.rem(s_id, steps_per_seq) == 0
    pen0 = jnp.where(first_in_seq, -jnp.inf, 0.0).astype(F32)

    @pl.when(s_id == 0)
    def _():
        z_ref[t - blk:t, off_k:off_k + 2 * d_kv] = jnp.zeros((blk, 2 * d_kv), BF16)

        batch = c_ref.shape[0]
        c_act = _silu(c_ref[...])
        c_act = jnp.concatenate([c_act, jnp.zeros((mod_ref.shape[0] - batch, d_model), F32)], axis=0).astype(BF16)
        mod_ref[...] = jnp.broadcast_to(bada_ref[...], mod_ref.shape)

        def ada_piece(rows, cols, piece):
            mod_ref[:, cols] += jnp.dot(c_act[:, rows], piece.astype(BF16), preferred_element_type=F32)

        def win_piece(rows, cols, piece):
            win_ref[rows, cols] = piece.astype(BF16)

        def wout_piece(rows, cols, piece):
            wout_ref[rows, cols] = piece.astype(BF16)

        _stream_weights([(wada_hbm, ada_piece), (win_hbm, win_piece), (wout_hbm, wout_piece)], stage_ref, sem_ref)

    lane2 = lax.broadcasted_iota(jnp.int32, (2 * blk, LANES), 1)
    lo2 = lane2 < HEAD_DIM
    row = lax.broadcasted_iota(jnp.int32, (blk, blk), 0)
    col = lax.broadcasted_iota(jnp.int32, (blk, blk), 1)
    tri = col <= row
    half = blk // 2
    lanes_h = pairs_per_kv * half
    key_h = lax.broadcasted_iota(jnp.int32, (half, lanes_h), 0)
    qry_h = lax.broadcasted_iota(jnp.int32, (half, lanes_h), 1) & (half - 1)
    cur_h = key_h <= qry_h
    first_row_h = key_h == 0
    lane_pair = lax.broadcasted_iota(jnp.int32, (1, lanes_h), 1) // half
    key_rows = [slice(0, blk + half), slice(half, 2 * blk)]
    n_keys = blk + half
    keycol = lax.broadcasted_iota(jnp.int32, (HEAD_DIM, n_keys), 1)
    scale = 1.0 / math.sqrt(HEAD_DIM)

    def modulated_norm():
        shift = mod_ref[pl.ds(b_in, 1), 0:d_model]
        gain = ng_ref[...] * (1.0 + mod_ref[pl.ds(b_in, 1), d_model:2 * d_model])
        for r in range(0, t, blk):
            x = x_ref[r:r + blk, :]
            ms = jnp.mean(x * x, axis=-1, keepdims=True)
            h_ref[r:r + blk, :] = ((x * lax.rsqrt(ms + EPS)) * gain + shift).astype(BF16)

    def in_panels(first, count):
        for n in range(first, first + count):
            cs = slice(n * MXU_WIDTH, (n + 1) * MXU_WIDTH)
            z_ref[:, cs] = jnp.dot(h_ref[...], win_ref[:, cs], preferred_element_type=F32).astype(BF16)

    def out_panels(first, count):
        for n in range(first, first + count):
            cs = slice(n * MXU_WIDTH, (n + 1) * MXU_WIDTH)
            y_ref[:, cs] = jnp.dot(a_ref[...], wout_ref[:, cs], preferred_element_type=F32)

    def finish(j):
        rows = slice(j * blk, (j + 1) * blk)
        xn = xp_ref[rows, :] + mod_ref[pl.ds(b_out, 1), 2 * d_model:3 * d_model] * y_ref[rows, :]
        ms = jnp.mean(xn * xn, axis=-1, keepdims=True)
        o_ref[rows, :] = (xn * lax.rsqrt(ms + EPS)) * fg_ref[...]

    kv_cache = {}

    def block_kv(j):
        if j in kv_cache:
            return kv_cache[j]
        r0 = j * blk
        rows = slice(r0, r0 + blk)
        k_cur = z_ref[rows, off_k:off_k + d_kv]
        v_cur = z_ref[rows, off_v:off_v + d_kv]
        if j == 0:
            k_prev = kvp_ref[:, 0:d_kv]
            v_prev = kvp_ref[:, d_kv:2 * d_kv]
        else:
            k_prev = z_ref[r0 - blk:r0, off_k:off_k + d_kv]
            v_prev = z_ref[r0 - blk:r0, off_v:off_v + d_kv]
        kext = jnp.concatenate([k_prev, k_cur], axis=0).astype(F32) * scale
        kroll = pltpu.roll(kext, HEAD_DIM, 1)
        zero2 = jnp.zeros_like(kext)
        k_lo = [jnp.where(lo2, kext, zero2), jnp.where(lo2, kroll, zero2)]
        k_hi = [jnp.where(lo2, zero2, kroll), jnp.where(lo2, zero2, kext)]
        vext = jnp.concatenate([v_prev, v_cur], axis=0).astype(F32)
        vext_t = [vext[rows_h, :].T for rows_h in key_rows]
        kv_cache[j] = (k_lo, k_hi, vext_t)
        return kv_cache[j]

    def attention(j, kv):
        r0 = j * blk
        k_lo, k_hi, vext_t = block_kv(j)
        p0 = kv * pairs_per_kv
        s_h = []
        for h in range(2):
            kbd = jnp.concatenate([k_lo[kv][key_rows[h]], k_hi[kv][key_rows[h]]], axis=0).astype(BF16)
            q_h = jnp.concatenate(
                [z_ref[r0 + h * half:r0 + (h + 1) * half, (p0 + p) * LANES:(p0 + p + 1) * LANES]
                 for p in range(pairs_per_kv)], axis=0)
            s_h.append(lax.dot_general(kbd, q_h, (((1,), (1,)), ((), ())), preferred_element_type=F32))
        yield
        ones = jnp.ones((16, n_keys), F32)
        att = {}
        for h in range(2):
            v_t = vext_t[h][kv * HEAD_DIM:(kv + 1) * HEAD_DIM, :]
            v_aug = jnp.concatenate([jnp.where(keycol == 0, 0.0, v_t), ones], axis=0).astype(BF16)
            pm = []
            for e in range(2):
                s = s_h[h][e * n_keys:(e + 1) * n_keys, :]
                s_a, s_b, s_c = s[0:half], s[half:blk], s[blk:n_keys]
                if h == 0:
                    if j == 0:
                        s_a, s_b = s_a + pen0, s_b + pen0
                    sc = jnp.concatenate([jnp.where(cur_h, s_c, s_a), s_b], axis=0)
                else:
                    if j == 0:
                        s_a = s_a + pen0
                    sc = jnp.concatenate([s_b, jnp.where(cur_h, s_c, s_a)], axis=0)
                sink = jnp.zeros((1, lanes_h), F32)
                for p in range(pairs_per_kv):
                    sink = jnp.where(lane_pair == p, sinks_ref[2 * (p0 + p) + e], sink)
                m = jnp.maximum(jnp.max(sc, axis=0, keepdims=True), sink)
                pr = jnp.exp(sc - m)
                p_sink = jnp.exp(sink - m)
                top, bot = pr[0:half], pr[half:blk]
                zero = jnp.zeros_like(top)
                if h == 0:
                    visible = [jnp.where(first_row_h, p_sink, jnp.where(cur_h, zero, top)), bot,
                               jnp.where(cur_h, top, zero)]
                else:
                    visible = [jnp.where(first_row_h, p_sink, jnp.where(cur_h, zero, bot)), top,
                               jnp.where(cur_h, bot, zero)]
                pm.append(jnp.concatenate(visible, axis=0).astype(BF16))

            for tl in range(pairs_per_kv // 2):
                cs = slice(tl * LANES, (tl + 1) * LANES)
                w = jnp.concatenate([pm[0][:, cs], pm[1][:, cs]], axis=1)
                o_t = jnp.dot(v_aug, w, preferred_element_type=F32)
                inv = 1.0 / o_t[HEAD_DIM:HEAD_DIM + 8, :]
                inv = jnp.concatenate([inv] * (HEAD_DIM // 8), axis=0)
                att_t = o_t[0:HEAD_DIM, :] * inv
                att[(h, tl)] = jnp.concatenate([att_t[:, 0:LANES], att_t[:, LANES:2 * LANES]], axis=0).T
        yield
        for (h, tl), a in att.items():
            rows = slice(r0 + h * half, r0 + (h + 1) * half)
            for u in range(2):
                c0 = (p0 + 2 * tl + u) * LANES
                g = z_ref[rows, off_ga + c0:off_ga + c0 + LANES].astype(F32)
                a_ref[rows, c0:c0 + LANES] = (a[u * half:(u + 1) * half, :] * _silu(g)).astype(BF16)

    def spatial_gating():
        vn = []
        for j in range(n_blk):
            vs = z_ref[j * blk:(j + 1) * blk, off_vs:off_vs + d_sgu].astype(F32)
            mu = jnp.mean(vs, axis=-1, keepdims=True)
            dv = vs - mu
            var = jnp.mean(dv * dv, axis=-1, keepdims=True)
            vn.append(((dv * lax.rsqrt(var + EPS)) * lng_ref[...] + lnb_ref[...]).astype(BF16))
        yield
        bias_t = sb_ref[...].T
        mixed = []
        for g in range(SGU_GROUPS):
            cs = slice(g * LANES, (g + 1) * LANES)
            w = jnp.where(tri, sw_ref[g], 0.0).astype(BF16)
            vn_g = jnp.concatenate([vn[j][:, cs] for j in range(n_blk)], axis=1)
            mixed.append(jnp.dot(w, vn_g, preferred_element_type=F32) + bias_t[:, g:g + 1])
        groups_per_panel = MXU_WIDTH // LANES
        for g0 in range(0, SGU_GROUPS, groups_per_panel):
            yield
            for g in range(g0, g0 + groups_per_panel):
                c0 = g * LANES
                for j in range(n_blk):
                    rows = slice(j * blk, (j + 1) * blk)
                    u = z_ref[rows, off_u + c0:off_u + c0 + LANES].astype(F32)
                    gg = z_ref[rows, off_gs + c0:off_gs + c0 + LANES].astype(F32)
                    a_ref[rows, d_attn + c0:d_attn + c0 + LANES] = (
                        (u * mixed[g][:, j * blk:(j + 1) * blk]) * _silu(gg)).astype(BF16)

    assert n_blk == 2 and N_KV_HEADS == 2, "the emission schedule below is written for two blocks per tile"
    panel = {name: (off // MXU_WIDTH, width // MXU_WIDTH) for name, off, width in (
        ("q", 0, d_attn), ("kv", off_k, 2 * d_kv), ("ga", off_ga, d_attn),
        ("u", off_u, d_sgu), ("vs", off_vs, d_sgu), ("gs", off_gs, d_sgu))}
    assert all(c >= 1 for _, c in panel.values()) and d_in % MXU_WIDTH == 0

    n_out = d_model // MXU_WIDTH

    def step_body(first):
        kv_cache.clear()
        kvp_ref[...] = z_ref[t - blk:t, off_k:off_k + 2 * d_kv]

        def names(name):
            return list(range(panel[name][0], sum(panel[name])))

        gate_panels = [n for pair in zip(names("u"), names("gs")) for n in pair]
        rest = names("vs") + names("ga") + gate_panels

        def fill(count):
            for _ in range(min(count, len(rest))):
                in_panels(rest.pop(0), 1)

        if not first:
            out_panels(0, n_out // 2)
        modulated_norm()
        in_panels(*panel["q"])
        in_panels(*panel["kv"])
        fill(1)
        att = [attention(j, kv) for j in range(n_blk) for kv in range(N_KV_HEADS)]
        sgu = spatial_gating()
        for i, piece in enumerate(att):
            next(piece)
            if i == 0 and not first:
                out_panels(n_out // 2, n_out - n_out // 2)
            else:
                fill(4)
            next(piece)
            if i == 0 and not first:
                for j in range(n_blk):
                    finish(j)
            if i == 1:
                next(sgu)
            if i == 2:
                next(sgu)
        for piece in att:
            for _ in piece:
                pass
        next(sgu)
        while rest:
            fill(2)
            next(sgu, None)
        for _ in sgu:
            pass

    @pl.when(s_id == 0)
    def _():
        step_body(True)

    @pl.when((s_id > 0) & (s_id < n_tiles))
    def _():
        step_body(False)

    @pl.when(s_id == n_tiles)
    def _():
        out_panels(0, n_out)
        for j in range(n_blk):
            finish(j)


def _layer(sinks, x2, c, w_ada, b_ada, norm_g, w_in, ln_g, ln_b, sgu_w, sgu_b, w_out, final_g, seq, d_attn, d_sgu):
    m, d = x2.shape
    t = TILE_TOKENS
    d_in = w_in.shape[1]
    d_mix = w_out.shape[0]
    d_kv = N_KV_HEADS * HEAD_DIM
    n_tiles = m // t
    mod_rows = 8
    assert c.shape[0] <= mod_rows

    def in_tile(s):
        return jnp.minimum(s, n_tiles - 1)

    def out_tile(s):
        return jnp.maximum(s - 1, 0)

    const2 = lambda s: (0, 0)
    const3 = lambda s: (0, 0, 0)
    return pl.pallas_call(
        functools.partial(_layer_kernel, d_attn=d_attn, d_sgu=d_sgu, steps_per_seq=seq // t, n_tiles=n_tiles),
        grid=(n_tiles + 1,),
        in_specs=[
            pl.BlockSpec(memory_space=pltpu.SMEM),
            pl.BlockSpec((t, d), lambda s: (in_tile(s), 0)),
            pl.BlockSpec((t, d), lambda s: (out_tile(s), 0)),
            pl.BlockSpec(c.shape, const2),
            pl.BlockSpec(memory_space=pl.ANY),
            pl.BlockSpec(b_ada.shape, const2),
            pl.BlockSpec((1, d), const2),
            pl.BlockSpec(memory_space=pl.ANY),
            pl.BlockSpec((1, d_sgu), const2),
            pl.BlockSpec((1, d_sgu), const2),
            pl.BlockSpec((SGU_GROUPS, WINDOW, WINDOW), const3),
            pl.BlockSpec((SGU_GROUPS, WINDOW), const2),
            pl.BlockSpec(memory_space=pl.ANY),
            pl.BlockSpec((1, d), const2),
        ],
        out_specs=pl.BlockSpec((t, d), lambda s: (out_tile(s), 0)),
        out_shape=jax.ShapeDtypeStruct((m, d), F32),
        scratch_shapes=[
            pltpu.VMEM((mod_rows, w_ada.shape[1]), F32),
            pltpu.VMEM((d, d_in), BF16),
            pltpu.VMEM((d_mix, d + LANES), BF16),
            pltpu.VMEM((STAGE_SLOTS, STAGE_ROWS, STAGE_COLS), F32),
            pltpu.SemaphoreType.DMA((STAGE_SLOTS,)),
            pltpu.VMEM((t, d), BF16),
            pltpu.VMEM((t, d_in), BF16),
            pltpu.VMEM((WINDOW, 2 * d_kv), BF16),
            pltpu.VMEM((t, d_mix), BF16),
            pltpu.VMEM((t, d), F32),
        ],
        compiler_params=pltpu.CompilerParams(
            dimension_semantics=("arbitrary",), vmem_limit_bytes=VMEM_LIMIT_BYTES),
        name="layer",
    )(sinks, x2, x2, c, w_ada, b_ada, norm_g, w_in, ln_g, ln_b, sgu_w, sgu_b, w_out, final_g)


def kernel(x, c, norm_g, w_ada, b_ada, w_in, attn_sinks, sgu_ln_g, sgu_ln_b, sgu_w, sgu_b, w_out, final_g):
    batch, seq, d = x.shape
    depth = norm_g.shape[0]
    n_q_heads = attn_sinks.shape[1]
    d_attn = n_q_heads * HEAD_DIM
    d_sgu = sgu_ln_g.shape[1]
    assert sgu_w.shape[1:] == (SGU_GROUPS, WINDOW, WINDOW) and d_sgu == SGU_GROUPS * LANES
    assert depth == 1, "stacked layers need the un-normalised residual stream between layers"

    x2 = x.reshape(batch * seq, d)
    out = _layer(attn_sinks[0], x2, c, w_ada[0], b_ada[0].reshape(1, -1), norm_g[0].reshape(1, d), w_in[0],
                 sgu_ln_g[0].reshape(1, d_sgu), sgu_ln_b[0].reshape(1, d_sgu), sgu_w[0], sgu_b[0],
                 w_out[0], final_g.reshape(1, d), seq, d_attn, d_sgu)
    return out.reshape(batch, seq, d)
```

```python
import functools
import math

import jax
import jax.numpy as jnp
from jax import lax
from jax.experimental import pallas as pl
from jax.experimental.pallas import tpu as pltpu

HEAD_DIM = 64
N_KV_HEADS = 2
WINDOW = 128
SGU_GROUPS = 8
EPS = 1e-6

F32 = jnp.float32
BF16 = jnp.bfloat16

LANES = 128
MXU_WIDTH = 256
VMEM_LIMIT_BYTES = 60 * 1024 * 1024
TILE_TOKENS = 256
STAGE_ROWS = 128
STAGE_COLS = 3072
STAGE_SLOTS = 4


def _silu(v):
    return v / (1.0 + jnp.exp(-v))


def _stream_weights(streams, stage_ref, sem_ref):
    ahead = STAGE_SLOTS - 1
    plan, total = [], 0
    for w_hbm, consume in streams:
        k, n = w_hbm.shape
        col_parts = pl.cdiv(n, STAGE_COLS)
        assert k % STAGE_ROWS == 0 and n % col_parts == 0 and (n // col_parts) % LANES == 0
        plan.append((w_hbm, consume, k // STAGE_ROWS, col_parts, n // col_parts, total))
        total += (k // STAGE_ROWS) * col_parts

    def copy(s, i, p):
        w_hbm, _, _, col_parts, width, base = plan[s]
        if isinstance(i, int):
            slot, r0 = (base + i * col_parts + p) % STAGE_SLOTS, i * STAGE_ROWS
        else:
            slot, r0 = lax.rem(base + i * col_parts + p, STAGE_SLOTS), pl.multiple_of(i * STAGE_ROWS, STAGE_ROWS)
        return pltpu.make_async_copy(
            w_hbm.at[pl.ds(r0, STAGE_ROWS), pl.ds(p * width, width)],
            stage_ref.at[slot, :, pl.ds(0, width)], sem_ref.at[slot]), slot

    def locate(j):
        for s, (_, _, n_blocks, col_parts, _, base) in enumerate(plan):
            if j < base + n_blocks * col_parts:
                return (s,) + divmod(j - base, col_parts)
        return None

    def receive(s, i, p):
        _, consume, _, _, width, _ = plan[s]
        piece_copy, slot = copy(s, i, p)
        piece_copy.wait()
        consume(i, slice(p * width, (p + 1) * width), stage_ref[slot, :, 0:width])

    for j in range(min(ahead, total)):
        copy(*locate(j))[0].start()
    for s, (_, _, n_blocks, col_parts, _, base) in enumerate(plan):
        tail_blocks = min(n_blocks, pl.cdiv(col_parts - 1 + ahead, col_parts))

        def row_block(i, carry, s=s, col_parts=col_parts):
            for p in range(col_parts):
                di, p_ahead = divmod(p + ahead, col_parts)
                copy(s, i + di, p_ahead)[0].start()
                receive(s, i, p)
            return carry

        lax.fori_loop(0, n_blocks - tail_blocks, row_block, None)
        for i in range(n_blocks - tail_blocks, n_blocks):
            for p in range(col_parts):
                nxt = locate(base + i * col_parts + p + ahead)
                if nxt is not None:
                    copy(*nxt)[0].start()
                receive(s, i, p)


def _layer_kernel(sinks_ref, x_ref, xp_ref, c_ref, wada_hbm, bada_ref, ng_ref, win_hbm, lng_ref, lnb_ref,
                  sw_ref, sb_ref, wout_hbm, fg_ref, o_ref,
                  mod_ref, cact_ref, win_ref, wout_ref, stage_ref, sem_ref, h_ref, z_ref, kvp_ref, a_ref, y_ref,
                  *, d_attn, d_sgu, steps_per_seq, n_tiles):
    t, d_model = x_ref.shape
    s_id = pl.program_id(0)
    blk = WINDOW
    n_blk = t // blk
    d_kv = N_KV_HEADS * HEAD_DIM
    pairs_per_kv = d_attn // LANES // N_KV_HEADS
    assert pairs_per_kv % 2 == 0

    off_k = d_attn
    off_v = off_k + d_kv
    off_ga = off_v + d_kv
    off_u = off_ga + d_attn
    off_vs = off_u + d_sgu
    off_gs = off_vs + d_sgu
    d_in = off_gs + d_sgu

    b_in = jnp.minimum(s_id, n_tiles - 1) // steps_per_seq
    b_out = jnp.maximum(s_id - 1, 0) // steps_per_seq
    first_in_seq = lax.rem(s_id, steps_per_seq) == 0
    pen0 = jnp.where(first_in_seq, -jnp.inf, 0.0).astype(F32)

    @pl.when(s_id == 0)
    def _():
        a_ref[...] = jnp.zeros_like(a_ref)
        z_ref[t - blk:t, off_k:off_k + 2 * d_kv] = jnp.zeros((blk, 2 * d_kv), BF16)

        batch = c_ref.shape[0]
        c_act = _silu(c_ref[...])
        c_act = jnp.concatenate([c_act, jnp.zeros((mod_ref.shape[0] - batch, d_model), F32)], axis=0)
        for i in range(d_model // STAGE_ROWS):
            cact_ref[i] = c_act[:, i * STAGE_ROWS:(i + 1) * STAGE_ROWS]
        mod_ref[...] = jnp.broadcast_to(bada_ref[...], mod_ref.shape)

        def block_rows(i):
            return pl.ds(i * STAGE_ROWS if isinstance(i, int) else pl.multiple_of(i * STAGE_ROWS, STAGE_ROWS),
                         STAGE_ROWS)

        def ada_piece(i, cols, piece):
            mod_ref[:, cols] += jnp.dot(cact_ref[i].astype(BF16), piece.astype(BF16), preferred_element_type=F32)

        def win_piece(i, cols, piece):
            win_ref[block_rows(i), cols] = piece.astype(BF16)

        def wout_piece(i, cols, piece):
            wout_ref[block_rows(i), cols] = piece.astype(BF16)

        _stream_weights([(wada_hbm, ada_piece), (win_hbm, win_piece), (wout_hbm, wout_piece)], stage_ref, sem_ref)

    lane2 = lax.broadcasted_iota(jnp.int32, (2 * blk, LANES), 1)
    lo2 = lane2 < HEAD_DIM
    row = lax.broadcasted_iota(jnp.int32, (blk, blk), 0)
    col = lax.broadcasted_iota(jnp.int32, (blk, blk), 1)
    tri = col <= row
    half = blk // 2
    lanes_h = pairs_per_kv * half
    key_h = lax.broadcasted_iota(jnp.int32, (half, lanes_h), 0)
    qry_h = lax.broadcasted_iota(jnp.int32, (half, lanes_h), 1) & (half - 1)
    cur_h = key_h <= qry_h
    first_row_h = key_h == 0
    lane_pair = lax.broadcasted_iota(jnp.int32, (1, lanes_h), 1) // half
    key_rows = [slice(0, blk + half), slice(half, 2 * blk)]
    n_keys = blk + half
    keycol = lax.broadcasted_iota(jnp.int32, (HEAD_DIM, n_keys), 1)
    scale = 1.0 / math.sqrt(HEAD_DIM)

    def modulated_norm():
        shift = mod_ref[pl.ds(b_in, 1), 0:d_model]
        gain = ng_ref[...] * (1.0 + mod_ref[pl.ds(b_in, 1), d_model:2 * d_model])
        for r in range(0, t, blk):
            x = x_ref[r:r + blk, :]
            ms = jnp.mean(x * x, axis=-1, keepdims=True)
            h_ref[r:r + blk, :] = ((x * lax.rsqrt(ms + EPS)) * gain + shift).astype(BF16)

    def in_panels(first, count):
        for n in range(first, first + count):
            cs = slice(n * MXU_WIDTH, (n + 1) * MXU_WIDTH)
            z_ref[:, cs] = jnp.dot(h_ref[...], win_ref[:, cs], preferred_element_type=F32).astype(BF16)

    def out_panels(first, count):
        for n in range(first, first + count):
            cs = slice(n * MXU_WIDTH, (n + 1) * MXU_WIDTH)
            y_ref[:, cs] = jnp.dot(a_ref[...], wout_ref[:, cs], preferred_element_type=F32)

    def finish(j):
        rows = slice(j * blk, (j + 1) * blk)
        xn = xp_ref[rows, :] + mod_ref[pl.ds(b_out, 1), 2 * d_model:3 * d_model] * y_ref[rows, :]
        ms = jnp.mean(xn * xn, axis=-1, keepdims=True)
        o_ref[rows, :] = (xn * lax.rsqrt(ms + EPS)) * fg_ref[...]

    kv_cache = {}

    def block_kv(j):
        if j in kv_cache:
            return kv_cache[j]
        r0 = j * blk
        rows = slice(r0, r0 + blk)
        k_cur = z_ref[rows, off_k:off_k + d_kv]
        v_cur = z_ref[rows, off_v:off_v + d_kv]
        if j == 0:
            k_prev = kvp_ref[:, 0:d_kv]
            v_prev = kvp_ref[:, d_kv:2 * d_kv]
        else:
            k_prev = z_ref[r0 - blk:r0, off_k:off_k + d_kv]
            v_prev = z_ref[r0 - blk:r0, off_v:off_v + d_kv]
        kext = jnp.concatenate([k_prev, k_cur], axis=0).astype(F32) * scale
        kroll = pltpu.roll(kext, HEAD_DIM, 1)
        zero2 = jnp.zeros_like(kext)
        k_lo = [jnp.where(lo2, kext, zero2), jnp.where(lo2, kroll, zero2)]
        k_hi = [jnp.where(lo2, zero2, kroll), jnp.where(lo2, zero2, kext)]
        vext = jnp.concatenate([v_prev, v_cur], axis=0).astype(F32)
        vext_t = [vext[rows_h, :].T for rows_h in key_rows]
        kv_cache[j] = (k_lo, k_hi, vext_t)
        return kv_cache[j]

    def attention(j, kv):
        r0 = j * blk
        k_lo, k_hi, vext_t = block_kv(j)
        p0 = kv * pairs_per_kv
        s_h = []
        for h in range(2):
            kbd = jnp.concatenate([k_lo[kv][key_rows[h]], k_hi[kv][key_rows[h]]], axis=0).astype(BF16)
            q_h = jnp.concatenate(
                [z_ref[r0 + h * half:r0 + (h + 1) * half, (p0 + p) * LANES:(p0 + p + 1) * LANES]
                 for p in range(pairs_per_kv)], axis=0)
            s_h.append(lax.dot_general(kbd, q_h, (((1,), (1,)), ((), ())), preferred_element_type=F32))
        yield
        ones = jnp.ones((16, n_keys), F32)
        att = {}
        for h in range(2):
            v_t = vext_t[h][kv * HEAD_DIM:(kv + 1) * HEAD_DIM, :]
            v_aug = jnp.concatenate([jnp.where(keycol == 0, 0.0, v_t), ones], axis=0).astype(BF16)
            pm = []
            for e in range(2):
                s = s_h[h][e * n_keys:(e + 1) * n_keys, :]
                s_a, s_b, s_c = s[0:half], s[half:blk], s[blk:n_keys]
                if h == 0:
                    if j == 0:
                        s_a, s_b = s_a + pen0, s_b + pen0
                    sc = jnp.concatenate([jnp.where(cur_h, s_c, s_a), s_b], axis=0)
                else:
                    if j == 0:
                        s_a = s_a + pen0
                    sc = jnp.concatenate([s_b, jnp.where(cur_h, s_c, s_a)], axis=0)
                sink = jnp.zeros((1, lanes_h), F32)
                for p in range(pairs_per_kv):
                    sink = jnp.where(lane_pair == p, sinks_ref[2 * (p0 + p) + e], sink)
                m = jnp.maximum(jnp.max(sc, axis=0, keepdims=True), sink)
                pr = jnp.exp(sc - m)
                p_sink = jnp.exp(sink - m)
                top, bot = pr[0:half], pr[half:blk]
                zero = jnp.zeros_like(top)
                if h == 0:
                    visible = [jnp.where(first_row_h, p_sink, jnp.where(cur_h, zero, top)), bot,
                               jnp.where(cur_h, top, zero)]
                else:
                    visible = [jnp.where(first_row_h, p_sink, jnp.where(cur_h, zero, bot)), top,
                               jnp.where(cur_h, bot, zero)]
                pm.append(jnp.concatenate(visible, axis=0).astype(BF16))

            for tl in range(pairs_per_kv // 2):
                cs = slice(tl * LANES, (tl + 1) * LANES)
                w = jnp.concatenate([pm[0][:, cs], pm[1][:, cs]], axis=1)
                o_t = jnp.dot(v_aug, w, preferred_element_type=F32)
                inv = 1.0 / o_t[HEAD_DIM:HEAD_DIM + 8, :]
                inv = jnp.concatenate([inv] * (HEAD_DIM // 8), axis=0)
                att_t = o_t[0:HEAD_DIM, :] * inv
                att[(h, tl)] = jnp.concatenate([att_t[:, 0:LANES], att_t[:, LANES:2 * LANES]], axis=0).T
        yield
        for (h, tl), a in att.items():
            rows = slice(r0 + h * half, r0 + (h + 1) * half)
            for u in range(2):
                c0 = (p0 + 2 * tl + u) * LANES
                g = z_ref[rows, off_ga + c0:off_ga + c0 + LANES].astype(F32)
                a_ref[rows, c0:c0 + LANES] = (a[u * half:(u + 1) * half, :] * _silu(g)).astype(BF16)

    def spatial_gating():
        vn = []
        for j in range(n_blk):
            vs = z_ref[j * blk:(j + 1) * blk, off_vs:off_vs + d_sgu].astype(F32)
            mu = jnp.mean(vs, axis=-1, keepdims=True)
            dv = vs - mu
            var = jnp.mean(dv * dv, axis=-1, keepdims=True)
            vn.append(((dv * lax.rsqrt(var + EPS)) * lng_ref[...] + lnb_ref[...]).astype(BF16))
        yield
        bias_t = sb_ref[...].T
        mixed = []
        for g in range(SGU_GROUPS):
            cs = slice(g * LANES, (g + 1) * LANES)
            w = jnp.where(tri, sw_ref[g], 0.0).astype(BF16)
            vn_g = jnp.concatenate([vn[j][:, cs] for j in range(n_blk)], axis=1)
            mixed.append(jnp.dot(w, vn_g, preferred_element_type=F32) + bias_t[:, g:g + 1])
        groups_per_panel = MXU_WIDTH // LANES
        for g0 in range(0, SGU_GROUPS, groups_per_panel):
            yield
            for g in range(g0, g0 + groups_per_panel):
                c0 = g * LANES
                for j in range(n_blk):
                    rows = slice(j * blk, (j + 1) * blk)
                    u = z_ref[rows, off_u + c0:off_u + c0 + LANES].astype(F32)
                    gg = z_ref[rows, off_gs + c0:off_gs + c0 + LANES].astype(F32)
                    a_ref[rows, d_attn + c0:d_attn + c0 + LANES] = (
                        (u * mixed[g][:, j * blk:(j + 1) * blk]) * _silu(gg)).astype(BF16)

    assert n_blk == 2 and N_KV_HEADS == 2, "the emission schedule below is written for two blocks per tile"
    panel = {name: (off // MXU_WIDTH, width // MXU_WIDTH) for name, off, width in (
        ("q", 0, d_attn), ("kv", off_k, 2 * d_kv), ("ga", off_ga, d_attn),
        ("u", off_u, d_sgu), ("vs", off_vs, d_sgu), ("gs", off_gs, d_sgu))}
    assert all(c >= 1 for _, c in panel.values()) and d_in % MXU_WIDTH == 0

    n_out = d_model // MXU_WIDTH

    @pl.when(s_id < n_tiles)
    def _():
        kvp_ref[...] = z_ref[t - blk:t, off_k:off_k + 2 * d_kv]
        def names(name):
            return list(range(panel[name][0], sum(panel[name])))

        gate_panels = [n for pair in zip(names("u"), names("gs")) for n in pair]
        rest = names("vs") + names("ga") + gate_panels

        def fill(count):
            for _ in range(min(count, len(rest))):
                in_panels(rest.pop(0), 1)

        out_panels(0, n_out // 2)
        modulated_norm()
        in_panels(*panel["q"])
        in_panels(*panel["kv"])
        fill(1)
        att = [attention(j, kv) for j in range(n_blk) for kv in range(N_KV_HEADS)]
        sgu = spatial_gating()
        for i, piece in enumerate(att):
            next(piece)
            if i == 0:
                out_panels(n_out // 2, n_out - n_out // 2)
            else:
                fill(4)
            next(piece)
            if i == 0:
                for j in range(n_blk):
                    finish(j)
            if i == 1:
                next(sgu)
            if i == 2:
                next(sgu)
        for piece in att:
            for _ in piece:
                pass
        next(sgu)
        while rest:
            fill(2)
            next(sgu, None)
        for _ in sgu:
            pass

    @pl.when(s_id == n_tiles)
    def _():
        out_panels(0, n_out)
        for j in range(n_blk):
            finish(j)


def _layer(sinks, x2, c, w_ada, b_ada, norm_g, w_in, ln_g, ln_b, sgu_w, sgu_b, w_out, final_g, seq, d_attn, d_sgu):
    m, d = x2.shape
    t = TILE_TOKENS
    d_in = w_in.shape[1]
    d_mix = w_out.shape[0]
    d_kv = N_KV_HEADS * HEAD_DIM
    n_tiles = m // t
    mod_rows = 8
    assert c.shape[0] <= mod_rows

    def in_tile(s):
        return jnp.minimum(s, n_tiles - 1)

    def out_tile(s):
        return jnp.maximum(s - 1, 0)

    const2 = lambda s: (0, 0)
    const3 = lambda s: (0, 0, 0)
    return pl.pallas_call(
        functools.partial(_layer_kernel, d_attn=d_attn, d_sgu=d_sgu, steps_per_seq=seq // t, n_tiles=n_tiles),
        grid=(n_tiles + 1,),
        in_specs=[
            pl.BlockSpec(memory_space=pltpu.SMEM),
            pl.BlockSpec((t, d), lambda s: (in_tile(s), 0)),
            pl.BlockSpec((t, d), lambda s: (out_tile(s), 0)),
            pl.BlockSpec(c.shape, const2),
            pl.BlockSpec(memory_space=pl.ANY),
            pl.BlockSpec(b_ada.shape, const2),
            pl.BlockSpec((1, d), const2),
            pl.BlockSpec(memory_space=pl.ANY),
            pl.BlockSpec((1, d_sgu), const2),
            pl.BlockSpec((1, d_sgu), const2),
            pl.BlockSpec((SGU_GROUPS, WINDOW, WINDOW), const3),
            pl.BlockSpec((SGU_GROUPS, WINDOW), const2),
            pl.BlockSpec(memory_space=pl.ANY),
            pl.BlockSpec((1, d), const2),
        ],
        out_specs=pl.BlockSpec((t, d), lambda s: (out_tile(s), 0)),
        out_shape=jax.ShapeDtypeStruct((m, d), F32),
        scratch_shapes=[
            pltpu.VMEM((mod_rows, w_ada.shape[1]), F32),
            pltpu.VMEM((d // STAGE_ROWS, mod_rows, STAGE_ROWS), F32),
            pltpu.VMEM((d, d_in), BF16),
            pltpu.VMEM((d_mix, d + LANES), BF16),
            pltpu.VMEM((STAGE_SLOTS, STAGE_ROWS, STAGE_COLS), F32),
            pltpu.SemaphoreType.DMA((STAGE_SLOTS,)),
            pltpu.VMEM((t, d), BF16),
            pltpu.VMEM((t, d_in), BF16),
            pltpu.VMEM((WINDOW, 2 * d_kv), BF16),
            pltpu.VMEM((t, d_mix), BF16),
            pltpu.VMEM((t, d), F32),
        ],
        compiler_params=pltpu.CompilerParams(
            dimension_semantics=("arbitrary",), vmem_limit_bytes=VMEM_LIMIT_BYTES),
        name="layer",
    )(sinks, x2, x2, c, w_ada, b_ada, norm_g, w_in, ln_g, ln_b, sgu_w, sgu_b, w_out, final_g)


def kernel(x, c, norm_g, w_ada, b_ada, w_in, attn_sinks, sgu_ln_g, sgu_ln_b, sgu_w, sgu_b, w_out, final_g):
    batch, seq, d = x.shape
    depth = norm_g.shape[0]
    n_q_heads = attn_sinks.shape[1]
    d_attn = n_q_heads * HEAD_DIM
    d_sgu = sgu_ln_g.shape[1]
    assert sgu_w.shape[1:] == (SGU_GROUPS, WINDOW, WINDOW) and d_sgu == SGU_GROUPS * LANES
    assert depth == 1, "stacked layers need the un-normalised residual stream between layers"

    x2 = x.reshape(batch * seq, d)
    out = _layer(attn_sinks[0], x2, c, w_ada[0], b_ada[0].reshape(1, -1), norm_g[0].reshape(1, d), w_in[0],
                 sgu_ln_g[0].reshape(1, d_sgu), sgu_ln_b[0].reshape(1, d_sgu), sgu_w[0], sgu_b[0],
                 w_out[0], final_g.reshape(1, d), seq, d_attn, d_sgu)
    return out.reshape(batch, seq, d)
```

```python
import functools
import math

import jax
import jax.numpy as jnp
from jax import lax
from jax.experimental import pallas as pl
from jax.experimental.pallas import tpu as pltpu

HEAD_DIM = 64
N_KV_HEADS = 2
WINDOW = 128
SGU_GROUPS = 8
EPS = 1e-6

F32 = jnp.float32
BF16 = jnp.bfloat16

LANES = 128
MXU_WIDTH = 256
VMEM_LIMIT_BYTES = 60 * 1024 * 1024
TILE_TOKENS = 256
STAGE_ROWS = 128
STAGE_COLS = 3072
STAGE_SLOTS = 4


def _silu(v):
    return v / (1.0 + jnp.exp(-v))


def _stream_weights(streams, stage_ref, sem_ref):
    pieces = []
    for w_hbm, consume in streams:
        k, n = w_hbm.shape
        col_parts = pl.cdiv(n, STAGE_COLS)
        width = n // col_parts
        assert k % STAGE_ROWS == 0 and n % col_parts == 0 and width % LANES == 0
        pieces += [(w_hbm, consume, i * STAGE_ROWS, p * width, width)
                   for i in range(k // STAGE_ROWS) for p in range(col_parts)]

    def copy(j):
        w_hbm, _, r0, c0, width = pieces[j]
        slot = j % STAGE_SLOTS
        return pltpu.make_async_copy(
            w_hbm.at[pl.ds(r0, STAGE_ROWS), pl.ds(c0, width)],
            stage_ref.at[slot, :, pl.ds(0, width)], sem_ref.at[slot])

    ahead = STAGE_SLOTS - 1
    for j in range(min(ahead, len(pieces))):
        copy(j).start()
    for j, (_, consume, r0, c0, width) in enumerate(pieces):
        if j + ahead < len(pieces):
            copy(j + ahead).start()
        copy(j).wait()
        consume(slice(r0, r0 + STAGE_ROWS), slice(c0, c0 + width), stage_ref[j % STAGE_SLOTS, :, 0:width])


def _layer_kernel(sinks_ref, x_ref, xp_ref, c_ref, wada_hbm, bada_ref, ng_ref, win_hbm, lng_ref, lnb_ref,
                  sw_ref, sb_ref, wout_hbm, fg_ref, o_ref,
                  mod_ref, win_ref, wout_ref, stage_ref, sem_ref, h_ref, z_ref, kvp_ref, a_ref, y_ref,
                  *, d_attn, d_sgu, steps_per_seq, n_tiles):
    t, d_model = x_ref.shape
    s_id = pl.program_id(0)
    blk = WINDOW
    n_blk = t // blk
    d_kv = N_KV_HEADS * HEAD_DIM
    pairs_per_kv = d_attn // LANES // N_KV_HEADS
    assert pairs_per_kv % 2 == 0

    off_k = d_attn
    off_v = off_k + d_kv
    off_ga = off_v + d_kv
    off_u = off_ga + d_attn
    off_vs = off_u + d_sgu
    off_gs = off_vs + d_sgu
    d_in = off_gs + d_sgu

    b_in = jnp.minimum(s_id, n_tiles - 1) // steps_per_seq
    b_out = jnp.maximum(s_id - 1, 0) // steps_per_seq
    first_in_seq = lax.rem(s_id, steps_per_seq) == 0
    pen0 = jnp.where(first_in_seq, -jnp.inf, 0.0).astype(F32)

    @pl.when(s_id == 0)
    def _():
        a_ref[...] = jnp.zeros_like(a_ref)
        z_ref[t - blk:t, off_k:off_k + 2 * d_kv] = jnp.zeros((blk, 2 * d_kv), BF16)

        batch = c_ref.shape[0]
        c_act = _silu(c_ref[...])
        c_act = jnp.concatenate([c_act, jnp.zeros((mod_ref.shape[0] - batch, d_model), F32)], axis=0).astype(BF16)
        mod_ref[...] = jnp.broadcast_to(bada_ref[...], mod_ref.shape)

        def ada_piece(rows, cols, piece):
            mod_ref[:, cols] += jnp.dot(c_act[:, rows], piece.astype(BF16), preferred_element_type=F32)

        def win_piece(rows, cols, piece):
            win_ref[rows, cols] = piece.astype(BF16)

        def wout_piece(rows, cols, piece):
            wout_ref[rows, cols] = piece.astype(BF16)

        _stream_weights([(wada_hbm, ada_piece), (win_hbm, win_piece), (wout_hbm, wout_piece)], stage_ref, sem_ref)

    lane2 = lax.broadcasted_iota(jnp.int32, (2 * blk, LANES), 1)
    lo2 = lane2 < HEAD_DIM
    row = lax.broadcasted_iota(jnp.int32, (blk, blk), 0)
    col = lax.broadcasted_iota(jnp.int32, (blk, blk), 1)
    tri = col <= row
    half = blk // 2
    lanes_h = pairs_per_kv * half
    key_h = lax.broadcasted_iota(jnp.int32, (half, lanes_h), 0)
    qry_h = lax.broadcasted_iota(jnp.int32, (half, lanes_h), 1) & (half - 1)
    cur_h = key_h <= qry_h
    first_row_h = key_h == 0
    lane_pair = lax.broadcasted_iota(jnp.int32, (1, lanes_h), 1) // half
    key_rows = [slice(0, blk + half), slice(half, 2 * blk)]
    n_keys = blk + half
    keycol = lax.broadcasted_iota(jnp.int32, (HEAD_DIM, n_keys), 1)
    scale = 1.0 / math.sqrt(HEAD_DIM)

    def modulated_norm():
        shift = mod_ref[pl.ds(b_in, 1), 0:d_model]
        gain = ng_ref[...] * (1.0 + mod_ref[pl.ds(b_in, 1), d_model:2 * d_model])
        for r in range(0, t, blk):
            x = x_ref[r:r + blk, :]
            ms = jnp.mean(x * x, axis=-1, keepdims=True)
            h_ref[r:r + blk, :] = ((x * lax.rsqrt(ms + EPS)) * gain + shift).astype(BF16)

    def in_panels(first, count):
        for n in range(first, first + count):
            cs = slice(n * MXU_WIDTH, (n + 1) * MXU_WIDTH)
            z_ref[:, cs] = jnp.dot(h_ref[...], win_ref[:, cs], preferred_element_type=F32).astype(BF16)

    def out_panels(first, count):
        for n in range(first, first + count):
            cs = slice(n * MXU_WIDTH, (n + 1) * MXU_WIDTH)
            y_ref[:, cs] = jnp.dot(a_ref[...], wout_ref[:, cs], preferred_element_type=F32)

    def finish(j):
        rows = slice(j * blk, (j + 1) * blk)
        xn = xp_ref[rows, :] + mod_ref[pl.ds(b_out, 1), 2 * d_model:3 * d_model] * y_ref[rows, :]
        ms = jnp.mean(xn * xn, axis=-1, keepdims=True)
        o_ref[rows, :] = (xn * lax.rsqrt(ms + EPS)) * fg_ref[...]

    kv_cache = {}

    def block_kv(j):
        if j in kv_cache:
            return kv_cache[j]
        r0 = j * blk
        rows = slice(r0, r0 + blk)
        k_cur = z_ref[rows, off_k:off_k + d_kv]
        v_cur = z_ref[rows, off_v:off_v + d_kv]
        if j == 0:
            k_prev = kvp_ref[:, 0:d_kv]
            v_prev = kvp_ref[:, d_kv:2 * d_kv]
        else:
            k_prev = z_ref[r0 - blk:r0, off_k:off_k + d_kv]
            v_prev = z_ref[r0 - blk:r0, off_v:off_v + d_kv]
        kext = jnp.concatenate([k_prev, k_cur], axis=0).astype(F32) * scale
        kroll = pltpu.roll(kext, HEAD_DIM, 1)
        zero2 = jnp.zeros_like(kext)
        k_lo = [jnp.where(lo2, kext, zero2), jnp.where(lo2, kroll, zero2)]
        k_hi = [jnp.where(lo2, zero2, kroll), jnp.where(lo2, zero2, kext)]
        vext = jnp.concatenate([v_prev, v_cur], axis=0).astype(F32)
        vext_t = [vext[rows_h, :].T for rows_h in key_rows]
        kv_cache[j] = (k_lo, k_hi, vext_t)
        return kv_cache[j]

    def attention(j, kv):
        r0 = j * blk
        k_lo, k_hi, vext_t = block_kv(j)
        p0 = kv * pairs_per_kv
        s_h = []
        for h in range(2):
            kbd = jnp.concatenate([k_lo[kv][key_rows[h]], k_hi[kv][key_rows[h]]], axis=0).astype(BF16)
            q_h = jnp.concatenate(
                [z_ref[r0 + h * half:r0 + (h + 1) * half, (p0 + p) * LANES:(p0 + p + 1) * LANES]
                 for p in range(pairs_per_kv)], axis=0)
            s_h.append(lax.dot_general(kbd, q_h, (((1,), (1,)), ((), ())), preferred_element_type=F32))
        yield
        att = {}
        for h in range(2):
            v_t = vext_t[h][kv * HEAD_DIM:(kv + 1) * HEAD_DIM, :]
            v_t = jnp.where(keycol == 0, 0.0, v_t).astype(BF16)
            pm, denom = [], []
            for e in range(2):
                s = s_h[h][e * n_keys:(e + 1) * n_keys, :]
                s_a, s_b, s_c = s[0:half], s[half:blk], s[blk:n_keys]
                if h == 0:
                    if j == 0:
                        s_a, s_b = s_a + pen0, s_b + pen0
                    sc = jnp.concatenate([jnp.where(cur_h, s_c, s_a), s_b], axis=0)
                else:
                    if j == 0:
                        s_a = s_a + pen0
                    sc = jnp.concatenate([s_b, jnp.where(cur_h, s_c, s_a)], axis=0)
                sink = jnp.zeros((1, lanes_h), F32)
                for p in range(pairs_per_kv):
                    sink = jnp.where(lane_pair == p, sinks_ref[2 * (p0 + p) + e], sink)
                m = jnp.maximum(jnp.max(sc, axis=0, keepdims=True), sink)
                pr = jnp.exp(sc - m)
                p_sink = jnp.exp(sink - m)
                top, bot = pr[0:half], pr[half:blk]
                zero = jnp.zeros_like(top)
                if h == 0:
                    visible = [jnp.where(first_row_h, p_sink, jnp.where(cur_h, zero, top)), bot,
                               jnp.where(cur_h, top, zero)]
                else:
                    visible = [jnp.where(first_row_h, p_sink, jnp.where(cur_h, zero, bot)), top,
                               jnp.where(cur_h, bot, zero)]
                pm.append(jnp.concatenate(visible, axis=0).astype(BF16))
                denom.append(sum(jnp.sum(part, axis=0, keepdims=True) for part in visible))

            for tl in range(pairs_per_kv // 2):
                cs = slice(tl * LANES, (tl + 1) * LANES)
                w = jnp.concatenate([pm[0][:, cs], pm[1][:, cs]], axis=1)
                o_t = jnp.dot(v_t, w, preferred_element_type=F32)
                att_t = o_t * (1.0 / jnp.concatenate([denom[0][:, cs], denom[1][:, cs]], axis=1))
                att[(h, tl)] = jnp.concatenate([att_t[:, 0:LANES], att_t[:, LANES:2 * LANES]], axis=0).T
        yield
        for (h, tl), a in att.items():
            rows = slice(r0 + h * half, r0 + (h + 1) * half)
            for u in range(2):
                c0 = (p0 + 2 * tl + u) * LANES
                g = z_ref[rows, off_ga + c0:off_ga + c0 + LANES].astype(F32)
                a_ref[rows, c0:c0 + LANES] = (a[u * half:(u + 1) * half, :] * _silu(g)).astype(BF16)

    def spatial_gating():
        vn = []
        for j in range(n_blk):
            vs = z_ref[j * blk:(j + 1) * blk, off_vs:off_vs + d_sgu].astype(F32)
            mu = jnp.mean(vs, axis=-1, keepdims=True)
            dv = vs - mu
            var = jnp.mean(dv * dv, axis=-1, keepdims=True)
            vn.append(((dv * lax.rsqrt(var + EPS)) * lng_ref[...] + lnb_ref[...]).astype(BF16))
        yield
        bias_t = sb_ref[...].T
        mixed = []
        for g in range(SGU_GROUPS):
            cs = slice(g * LANES, (g + 1) * LANES)
            w = jnp.where(tri, sw_ref[g], 0.0).astype(BF16)
            vn_g = jnp.concatenate([vn[j][:, cs] for j in range(n_blk)], axis=1)
            mixed.append(jnp.dot(w, vn_g, preferred_element_type=F32) + bias_t[:, g:g + 1])
        groups_per_panel = MXU_WIDTH // LANES
        for g0 in range(0, SGU_GROUPS, groups_per_panel):
            yield
            for g in range(g0, g0 + groups_per_panel):
                c0 = g * LANES
                for j in range(n_blk):
                    rows = slice(j * blk, (j + 1) * blk)
                    u = z_ref[rows, off_u + c0:off_u + c0 + LANES].astype(F32)
                    gg = z_ref[rows, off_gs + c0:off_gs + c0 + LANES].astype(F32)
                    a_ref[rows, d_attn + c0:d_attn + c0 + LANES] = (
                        (u * mixed[g][:, j * blk:(j + 1) * blk]) * _silu(gg)).astype(BF16)

    assert n_blk == 2 and N_KV_HEADS == 2, "the emission schedule below is written for two blocks per tile"
    panel = {name: (off // MXU_WIDTH, width // MXU_WIDTH) for name, off, width in (
        ("q", 0, d_attn), ("kv", off_k, 2 * d_kv), ("ga", off_ga, d_attn),
        ("u", off_u, d_sgu), ("vs", off_vs, d_sgu), ("gs", off_gs, d_sgu))}
    assert all(c >= 1 for _, c in panel.values()) and d_in % MXU_WIDTH == 0

    n_out = d_model // MXU_WIDTH

    @pl.when(s_id < n_tiles)
    def _():
        kvp_ref[...] = z_ref[t - blk:t, off_k:off_k + 2 * d_kv]
        def names(name):
            return list(range(panel[name][0], sum(panel[name])))

        gate_panels = [n for pair in zip(names("u"), names("gs")) for n in pair]
        rest = names("vs") + names("ga") + gate_panels

        def fill(count):
            for _ in range(min(count, len(rest))):
                in_panels(rest.pop(0), 1)

        out_panels(0, n_out // 2)
        modulated_norm()
        in_panels(*panel["q"])
        in_panels(*panel["kv"])
        fill(1)
        att = [attention(j, kv) for j in range(n_blk) for kv in range(N_KV_HEADS)]
        sgu = spatial_gating()
        for i, piece in enumerate(att):
            next(piece)
            if i == 0:
                out_panels(n_out // 2, n_out - n_out // 2)
            else:
                fill(4)
            next(piece)
            if i == 0:
                for j in range(n_blk):
                    finish(j)
            if i == 1:
                next(sgu)
            if i == 2:
                next(sgu)
        for piece in att:
            for _ in piece:
                pass
        next(sgu)
        while rest:
            fill(2)
            next(sgu, None)
        for _ in sgu:
            pass

    @pl.when(s_id == n_tiles)
    def _():
        out_panels(0, n_out)
        for j in range(n_blk):
            finish(j)


def _layer(sinks, x2, c, w_ada, b_ada, norm_g, w_in, ln_g, ln_b, sgu_w, sgu_b, w_out, final_g, seq, d_attn, d_sgu):
    m, d = x2.shape
    t = TILE_TOKENS
    d_in = w_in.shape[1]
    d_mix = w_out.shape[0]
    d_kv = N_KV_HEADS * HEAD_DIM
    n_tiles = m // t
    mod_rows = 8
    assert c.shape[0] <= mod_rows

    def in_tile(s):
        return jnp.minimum(s, n_tiles - 1)

    def out_tile(s):
        return jnp.maximum(s - 1, 0)

    const2 = lambda s: (0, 0)
    const3 = lambda s: (0, 0, 0)
    return pl.pallas_call(
        functools.partial(_layer_kernel, d_attn=d_attn, d_sgu=d_sgu, steps_per_seq=seq // t, n_tiles=n_tiles),
        grid=(n_tiles + 1,),
        in_specs=[
            pl.BlockSpec(memory_space=pltpu.SMEM),
            pl.BlockSpec((t, d), lambda s: (in_tile(s), 0)),
            pl.BlockSpec((t, d), lambda s: (out_tile(s), 0)),
            pl.BlockSpec(c.shape, const2),
            pl.BlockSpec(memory_space=pl.ANY),
            pl.BlockSpec(b_ada.shape, const2),
            pl.BlockSpec((1, d), const2),
            pl.BlockSpec(memory_space=pl.ANY),
            pl.BlockSpec((1, d_sgu), const2),
            pl.BlockSpec((1, d_sgu), const2),
            pl.BlockSpec((SGU_GROUPS, WINDOW, WINDOW), const3),
            pl.BlockSpec((SGU_GROUPS, WINDOW), const2),
            pl.BlockSpec(memory_space=pl.ANY),
            pl.BlockSpec((1, d), const2),
        ],
        out_specs=pl.BlockSpec((t, d), lambda s: (out_tile(s), 0)),
        out_shape=jax.ShapeDtypeStruct((m, d), F32),
        scratch_shapes=[
            pltpu.VMEM((mod_rows, w_ada.shape[1]), F32),
            pltpu.VMEM((d, d_in), BF16),
            pltpu.VMEM((d_mix, d + LANES), BF16),
            pltpu.VMEM((STAGE_SLOTS, STAGE_ROWS, STAGE_COLS), F32),
            pltpu.SemaphoreType.DMA((STAGE_SLOTS,)),
            pltpu.VMEM((t, d), BF16),
            pltpu.VMEM((t, d_in), BF16),
            pltpu.VMEM((WINDOW, 2 * d_kv), BF16),
            pltpu.VMEM((t, d_mix), BF16),
            pltpu.VMEM((t, d), F32),
        ],
        compiler_params=pltpu.CompilerParams(
            dimension_semantics=("arbitrary",), vmem_limit_bytes=VMEM_LIMIT_BYTES),
        name="layer",
    )(sinks, x2, x2, c, w_ada, b_ada, norm_g, w_in, ln_g, ln_b, sgu_w, sgu_b, w_out, final_g)


def kernel(x, c, norm_g, w_ada, b_ada, w_in, attn_sinks, sgu_ln_g, sgu_ln_b, sgu_w, sgu_b, w_out, final_g):
    batch, seq, d = x.shape
    depth = norm_g.shape[0]
    n_q_heads = attn_sinks.shape[1]
    d_attn = n_q_heads * HEAD_DIM
    d_sgu = sgu_ln_g.shape[1]
    assert sgu_w.shape[1:] == (SGU_GROUPS, WINDOW, WINDOW) and d_sgu == SGU_GROUPS * LANES
    assert depth == 1, "stacked layers need the un-normalised residual stream between layers"

    x2 = x.reshape(batch * seq, d)
    out = _layer(attn_sinks[0], x2, c, w_ada[0], b_ada[0].reshape(1, -1), norm_g[0].reshape(1, d), w_in[0],
                 sgu_ln_g[0].reshape(1, d_sgu), sgu_ln_b[0].reshape(1, d_sgu), sgu_w[0], sgu_b[0],
                 w_out[0], final_g.reshape(1, d), seq, d_attn, d_sgu)
    return out.reshape(batch, seq, d)
```

```python
import functools
import math

import jax
import jax.numpy as jnp
from jax import lax
from jax.experimental import pallas as pl
from jax.experimental.pallas import tpu as pltpu

HEAD_DIM = 64
N_KV_HEADS = 2
WINDOW = 128
SGU_GROUPS = 8
EPS = 1e-6

F32 = jnp.float32
BF16 = jnp.bfloat16

LANES = 128
MXU_WIDTH = 256
VMEM_LIMIT_BYTES = 60 * 1024 * 1024
TILE_TOKENS = 256
STAGE_ROWS = 128
STAGE_COLS = 3072
STAGE_SLOTS = 4


def _silu(v):
    return v / (1.0 + jnp.exp(-v))


def _stream_weights(streams, stage_ref, sem_ref):
    pieces = []
    for w_hbm, consume in streams:
        k, n = w_hbm.shape
        col_parts = pl.cdiv(n, STAGE_COLS)
        width = n // col_parts
        assert k % STAGE_ROWS == 0 and n % col_parts == 0 and width % LANES == 0
        pieces += [(w_hbm, consume, i * STAGE_ROWS, p * width, width)
                   for i in range(k // STAGE_ROWS) for p in range(col_parts)]

    def copy(j):
        w_hbm, _, r0, c0, width = pieces[j]
        slot = j % STAGE_SLOTS
        return pltpu.make_async_copy(
            w_hbm.at[pl.ds(r0, STAGE_ROWS), pl.ds(c0, width)],
            stage_ref.at[slot, :, pl.ds(0, width)], sem_ref.at[slot])

    ahead = STAGE_SLOTS - 1
    for j in range(min(ahead, len(pieces))):
        copy(j).start()
    for j, (_, consume, r0, c0, width) in enumerate(pieces):
        if j + ahead < len(pieces):
            copy(j + ahead).start()
        copy(j).wait()
        consume(slice(r0, r0 + STAGE_ROWS), slice(c0, c0 + width), stage_ref[j % STAGE_SLOTS, :, 0:width])


def _layer_kernel(sinks_ref, x_hbm, c_ref, wada_hbm, bada_ref, ng_ref, win_hbm, lng_ref, lnb_ref,
                  sw_ref, sb_ref, wout_hbm, fg_ref, o_hbm,
                  mod_ref, win_ref, wout_ref, stage_ref, sem_ref, h_ref, z_ref, kvp_ref, a_ref, y_ref,
                  *, d_attn, d_sgu, steps_per_seq, n_tiles):
    t = TILE_TOKENS
    d_model = x_hbm.shape[1]
    d_kv = N_KV_HEADS * HEAD_DIM

    a_ref[...] = jnp.zeros_like(a_ref)
    z_ref[t - WINDOW:t, d_attn:d_attn + 2 * d_kv] = jnp.zeros((WINDOW, 2 * d_kv), BF16)

    batch = c_ref.shape[0]
    c_act = _silu(c_ref[...])
    c_act = jnp.concatenate([c_act, jnp.zeros((mod_ref.shape[0] - batch, d_model), F32)], axis=0).astype(BF16)
    mod_ref[...] = jnp.broadcast_to(bada_ref[...], mod_ref.shape)

    def ada_piece(rows, cols, piece):
        mod_ref[:, cols] += jnp.dot(c_act[:, rows], piece.astype(BF16), preferred_element_type=F32)

    def win_piece(rows, cols, piece):
        win_ref[rows, cols] = piece.astype(BF16)

    def wout_piece(rows, cols, piece):
        wout_ref[rows, cols] = piece.astype(BF16)

    _stream_weights([(wada_hbm, ada_piece), (win_hbm, win_piece), (wout_hbm, wout_piece)], stage_ref, sem_ref)

    tile = lambda index: pl.BlockSpec((t, d_model), lambda s: (index(s), 0))
    in_tile = lambda s: jnp.minimum(s, n_tiles - 1)
    out_tile = lambda s: jnp.maximum(s - 1, 0)
    pltpu.emit_pipeline(
        functools.partial(_tile_step, d_attn=d_attn, d_sgu=d_sgu, steps_per_seq=steps_per_seq, n_tiles=n_tiles),
        grid=(n_tiles + 1,),
        in_specs=[tile(in_tile), tile(out_tile)],
        out_specs=[tile(out_tile)],
    )(x_hbm, x_hbm, o_hbm,
      scratches=(sinks_ref, ng_ref, lng_ref, lnb_ref, sw_ref, sb_ref, fg_ref,
                 mod_ref, win_ref, wout_ref, h_ref, z_ref, kvp_ref, a_ref, y_ref))


def _tile_step(x_ref, xp_ref, o_ref, sinks_ref, ng_ref, lng_ref, lnb_ref, sw_ref, sb_ref, fg_ref,
               mod_ref, win_ref, wout_ref, h_ref, z_ref, kvp_ref, a_ref, y_ref,
               *, d_attn, d_sgu, steps_per_seq, n_tiles):
    t, d_model = x_ref.shape
    s_id = pl.program_id(0)
    blk = WINDOW
    n_blk = t // blk
    d_kv = N_KV_HEADS * HEAD_DIM
    pairs_per_kv = d_attn // LANES // N_KV_HEADS
    assert pairs_per_kv % 2 == 0

    off_k = d_attn
    off_v = off_k + d_kv
    off_ga = off_v + d_kv
    off_u = off_ga + d_attn
    off_vs = off_u + d_sgu
    off_gs = off_vs + d_sgu
    d_in = off_gs + d_sgu

    b_in = jnp.minimum(s_id, n_tiles - 1) // steps_per_seq
    b_out = jnp.maximum(s_id - 1, 0) // steps_per_seq
    first_in_seq = lax.rem(s_id, steps_per_seq) == 0
    pen0 = jnp.where(first_in_seq, -jnp.inf, 0.0).astype(F32)

    lane2 = lax.broadcasted_iota(jnp.int32, (2 * blk, LANES), 1)
    lo2 = lane2 < HEAD_DIM
    row = lax.broadcasted_iota(jnp.int32, (blk, blk), 0)
    col = lax.broadcasted_iota(jnp.int32, (blk, blk), 1)
    tri = col <= row
    half = blk // 2
    lanes_h = pairs_per_kv * half
    key_h = lax.broadcasted_iota(jnp.int32, (half, lanes_h), 0)
    qry_h = lax.broadcasted_iota(jnp.int32, (half, lanes_h), 1) & (half - 1)
    cur_h = key_h <= qry_h
    first_row_h = key_h == 0
    lane_pair = lax.broadcasted_iota(jnp.int32, (1, lanes_h), 1) // half
    key_rows = [slice(0, blk + half), slice(half, 2 * blk)]
    n_keys = blk + half
    keycol = lax.broadcasted_iota(jnp.int32, (HEAD_DIM, n_keys), 1)
    scale = 1.0 / math.sqrt(HEAD_DIM)

    def modulated_norm():
        shift = mod_ref[pl.ds(b_in, 1), 0:d_model]
        gain = ng_ref[...] * (1.0 + mod_ref[pl.ds(b_in, 1), d_model:2 * d_model])
        for r in range(0, t, blk):
            x = x_ref[r:r + blk, :]
            ms = jnp.mean(x * x, axis=-1, keepdims=True)
            h_ref[r:r + blk, :] = ((x * lax.rsqrt(ms + EPS)) * gain + shift).astype(BF16)

    def in_panels(first, count):
        for n in range(first, first + count):
            cs = slice(n * MXU_WIDTH, (n + 1) * MXU_WIDTH)
            z_ref[:, cs] = jnp.dot(h_ref[...], win_ref[:, cs], preferred_element_type=F32).astype(BF16)

    def out_panels(first, count):
        for n in range(first, first + count):
            cs = slice(n * MXU_WIDTH, (n + 1) * MXU_WIDTH)
            y_ref[:, cs] = jnp.dot(a_ref[...], wout_ref[:, cs], preferred_element_type=F32)

    def finish(j):
        rows = slice(j * blk, (j + 1) * blk)
        xn = xp_ref[rows, :] + mod_ref[pl.ds(b_out, 1), 2 * d_model:3 * d_model] * y_ref[rows, :]
        ms = jnp.mean(xn * xn, axis=-1, keepdims=True)
        o_ref[rows, :] = (xn * lax.rsqrt(ms + EPS)) * fg_ref[...]

    kv_cache = {}

    def block_kv(j):
        if j in kv_cache:
            return kv_cache[j]
        r0 = j * blk
        rows = slice(r0, r0 + blk)
        k_cur = z_ref[rows, off_k:off_k + d_kv]
        v_cur = z_ref[rows, off_v:off_v + d_kv]
        if j == 0:
            k_prev = kvp_ref[:, 0:d_kv]
            v_prev = kvp_ref[:, d_kv:2 * d_kv]
        else:
            k_prev = z_ref[r0 - blk:r0, off_k:off_k + d_kv]
            v_prev = z_ref[r0 - blk:r0, off_v:off_v + d_kv]
        kext = jnp.concatenate([k_prev, k_cur], axis=0).astype(F32) * scale
        kroll = pltpu.roll(kext, HEAD_DIM, 1)
        zero2 = jnp.zeros_like(kext)
        k_lo = [jnp.where(lo2, kext, zero2), jnp.where(lo2, kroll, zero2)]
        k_hi = [jnp.where(lo2, zero2, kroll), jnp.where(lo2, zero2, kext)]
        vext = jnp.concatenate([v_prev, v_cur], axis=0).astype(F32)
        vext_t = [vext[rows_h, :].T for rows_h in key_rows]
        kv_cache[j] = (k_lo, k_hi, vext_t)
        return kv_cache[j]

    def attention(j, kv):
        r0 = j * blk
        k_lo, k_hi, vext_t = block_kv(j)
        p0 = kv * pairs_per_kv
        s_h = []
        for h in range(2):
            kbd = jnp.concatenate([k_lo[kv][key_rows[h]], k_hi[kv][key_rows[h]]], axis=0).astype(BF16)
            q_h = jnp.concatenate(
                [z_ref[r0 + h * half:r0 + (h + 1) * half, (p0 + p) * LANES:(p0 + p + 1) * LANES]
                 for p in range(pairs_per_kv)], axis=0)
            s_h.append(lax.dot_general(kbd, q_h, (((1,), (1,)), ((), ())), preferred_element_type=F32))
        yield
        ones = jnp.ones((16, n_keys), F32)
        att = {}
        for h in range(2):
            v_t = vext_t[h][kv * HEAD_DIM:(kv + 1) * HEAD_DIM, :]
            v_aug = jnp.concatenate([jnp.where(keycol == 0, 0.0, v_t), ones], axis=0).astype(BF16)
            pm = []
            for e in range(2):
                s = s_h[h][e * n_keys:(e + 1) * n_keys, :]
                s_a, s_b, s_c = s[0:half], s[half:blk], s[blk:n_keys]
                if h == 0:
                    if j == 0:
                        s_a, s_b = s_a + pen0, s_b + pen0
                    sc = jnp.concatenate([jnp.where(cur_h, s_c, s_a), s_b], axis=0)
                else:
                    if j == 0:
                        s_a = s_a + pen0
                    sc = jnp.concatenate([s_b, jnp.where(cur_h, s_c, s_a)], axis=0)
                sink = jnp.zeros((1, lanes_h), F32)
                for p in range(pairs_per_kv):
                    sink = jnp.where(lane_pair == p, sinks_ref[2 * (p0 + p) + e], sink)
                m = jnp.maximum(jnp.max(sc, axis=0, keepdims=True), sink)
                pr = jnp.exp(sc - m)
                p_sink = jnp.exp(sink - m)
                top, bot = pr[0:half], pr[half:blk]
                zero = jnp.zeros_like(top)
                if h == 0:
                    visible = [jnp.where(first_row_h, p_sink, jnp.where(cur_h, zero, top)), bot,
                               jnp.where(cur_h, top, zero)]
                else:
                    visible = [jnp.where(first_row_h, p_sink, jnp.where(cur_h, zero, bot)), top,
                               jnp.where(cur_h, bot, zero)]
                pm.append(jnp.concatenate(visible, axis=0).astype(BF16))

            for tl in range(pairs_per_kv // 2):
                cs = slice(tl * LANES, (tl + 1) * LANES)
                w = jnp.concatenate([pm[0][:, cs], pm[1][:, cs]], axis=1)
                o_t = jnp.dot(v_aug, w, preferred_element_type=F32)
                inv = 1.0 / o_t[HEAD_DIM:HEAD_DIM + 8, :]
                inv = jnp.concatenate([inv] * (HEAD_DIM // 8), axis=0)
                att_t = o_t[0:HEAD_DIM, :] * inv
                att[(h, tl)] = jnp.concatenate([att_t[:, 0:LANES], att_t[:, LANES:2 * LANES]], axis=0).T
        yield
        for (h, tl), a in att.items():
            rows = slice(r0 + h * half, r0 + (h + 1) * half)
            for u in range(2):
                c0 = (p0 + 2 * tl + u) * LANES
                g = z_ref[rows, off_ga + c0:off_ga + c0 + LANES].astype(F32)
                a_ref[rows, c0:c0 + LANES] = (a[u * half:(u + 1) * half, :] * _silu(g)).astype(BF16)

    def spatial_gating():
        vn = []
        for j in range(n_blk):
            vs = z_ref[j * blk:(j + 1) * blk, off_vs:off_vs + d_sgu].astype(F32)
            mu = jnp.mean(vs, axis=-1, keepdims=True)
            dv = vs - mu
            var = jnp.mean(dv * dv, axis=-1, keepdims=True)
            vn.append(((dv * lax.rsqrt(var + EPS)) * lng_ref[...] + lnb_ref[...]).astype(BF16))
        yield
        bias_t = sb_ref[...].T
        mixed = []
        for g in range(SGU_GROUPS):
            cs = slice(g * LANES, (g + 1) * LANES)
            w = jnp.where(tri, sw_ref[g], 0.0).astype(BF16)
            vn_g = jnp.concatenate([vn[j][:, cs] for j in range(n_blk)], axis=1)
            mixed.append(jnp.dot(w, vn_g, preferred_element_type=F32) + bias_t[:, g:g + 1])
        groups_per_panel = MXU_WIDTH // LANES
        for g0 in range(0, SGU_GROUPS, groups_per_panel):
            yield
            for g in range(g0, g0 + groups_per_panel):
                c0 = g * LANES
                for j in range(n_blk):
                    rows = slice(j * blk, (j + 1) * blk)
                    u = z_ref[rows, off_u + c0:off_u + c0 + LANES].astype(F32)
                    gg = z_ref[rows, off_gs + c0:off_gs + c0 + LANES].astype(F32)
                    a_ref[rows, d_attn + c0:d_attn + c0 + LANES] = (
                        (u * mixed[g][:, j * blk:(j + 1) * blk]) * _silu(gg)).astype(BF16)

    assert n_blk == 2 and N_KV_HEADS == 2, "the emission schedule below is written for two blocks per tile"
    panel = {name: (off // MXU_WIDTH, width // MXU_WIDTH) for name, off, width in (
        ("q", 0, d_attn), ("kv", off_k, 2 * d_kv), ("ga", off_ga, d_attn),
        ("u", off_u, d_sgu), ("vs", off_vs, d_sgu), ("gs", off_gs, d_sgu))}
    assert all(c >= 1 for _, c in panel.values()) and d_in % MXU_WIDTH == 0

    n_out = d_model // MXU_WIDTH

    @pl.when(s_id < n_tiles)
    def _():
        kvp_ref[...] = z_ref[t - blk:t, off_k:off_k + 2 * d_kv]
        def names(name):
            return list(range(panel[name][0], sum(panel[name])))

        gate_panels = [n for pair in zip(names("u"), names("gs")) for n in pair]
        rest = names("vs") + names("ga") + gate_panels

        def fill(count):
            for _ in range(min(count, len(rest))):
                in_panels(rest.pop(0), 1)

        out_panels(0, n_out // 2)
        modulated_norm()
        in_panels(*panel["q"])
        in_panels(*panel["kv"])
        fill(1)
        att = [attention(j, kv) for j in range(n_blk) for kv in range(N_KV_HEADS)]
        sgu = spatial_gating()
        for i, piece in enumerate(att):
            next(piece)
            if i == 0:
                out_panels(n_out // 2, n_out - n_out // 2)
            else:
                fill(4)
            next(piece)
            if i == 0:
                for j in range(n_blk):
                    finish(j)
            if i == 1:
                next(sgu)
            if i == 2:
                next(sgu)
        for piece in att:
            for _ in piece:
                pass
        next(sgu)
        while rest:
            fill(2)
            next(sgu, None)
        for _ in sgu:
            pass

    @pl.when(s_id == n_tiles)
    def _():
        out_panels(0, n_out)
        for j in range(n_blk):
            finish(j)


def _layer(sinks, x2, c, w_ada, b_ada, norm_g, w_in, ln_g, ln_b, sgu_w, sgu_b, w_out, final_g, seq, d_attn, d_sgu):
    m, d = x2.shape
    t = TILE_TOKENS
    d_in = w_in.shape[1]
    d_mix = w_out.shape[0]
    d_kv = N_KV_HEADS * HEAD_DIM
    n_tiles = m // t
    mod_rows = 8
    assert c.shape[0] <= mod_rows

    assert m % t == 0 and seq % t == 0
    hbm = pl.BlockSpec(memory_space=pl.ANY)
    vmem = pl.BlockSpec(memory_space=pltpu.VMEM)
    return pl.pallas_call(
        functools.partial(_layer_kernel, d_attn=d_attn, d_sgu=d_sgu, steps_per_seq=seq // t, n_tiles=n_tiles),
        in_specs=[
            pl.BlockSpec(memory_space=pltpu.SMEM),
            hbm,
            vmem,
            hbm,
            vmem,
            vmem,
            hbm,
            vmem,
            vmem,
            vmem,
            vmem,
            hbm,
            vmem,
        ],
        out_specs=hbm,
        out_shape=jax.ShapeDtypeStruct((m, d), F32),
        scratch_shapes=[
            pltpu.VMEM((mod_rows, w_ada.shape[1]), F32),
            pltpu.VMEM((d, d_in), BF16),
            pltpu.VMEM((d_mix, d + LANES), BF16),
            pltpu.VMEM((STAGE_SLOTS, STAGE_ROWS, STAGE_COLS), F32),
            pltpu.SemaphoreType.DMA((STAGE_SLOTS,)),
            pltpu.VMEM((t, d), BF16),
            pltpu.VMEM((t, d_in), BF16),
            pltpu.VMEM((WINDOW, 2 * d_kv), BF16),
            pltpu.VMEM((t, d_mix), BF16),
            pltpu.VMEM((t, d), F32),
        ],
        compiler_params=pltpu.CompilerParams(vmem_limit_bytes=VMEM_LIMIT_BYTES),
        name="layer",
    )(sinks, x2, c, w_ada, b_ada, norm_g, w_in, ln_g, ln_b, sgu_w, sgu_b, w_out, final_g)


def kernel(x, c, norm_g, w_ada, b_ada, w_in, attn_sinks, sgu_ln_g, sgu_ln_b, sgu_w, sgu_b, w_out, final_g):
    batch, seq, d = x.shape
    depth = norm_g.shape[0]
    n_q_heads = attn_sinks.shape[1]
    d_attn = n_q_heads * HEAD_DIM
    d_sgu = sgu_ln_g.shape[1]
    assert sgu_w.shape[1:] == (SGU_GROUPS, WINDOW, WINDOW) and d_sgu == SGU_GROUPS * LANES
    assert depth == 1, "stacked layers need the un-normalised residual stream between layers"

    x2 = x.reshape(batch * seq, d)
    out = _layer(attn_sinks[0], x2, c, w_ada[0], b_ada[0].reshape(1, -1), norm_g[0].reshape(1, d), w_in[0],
                 sgu_ln_g[0].reshape(1, d_sgu), sgu_ln_b[0].reshape(1, d_sgu), sgu_w[0], sgu_b[0],
                 w_out[0], final_g.reshape(1, d), seq, d_attn, d_sgu)
    return out.reshape(batch, seq, d)
```
